```python
import jax, jax.numpy as jnp
from jax import lax
import numpy as np

D_MODEL = 1024
BATCH = 8
SEQ = 2048
DEPTH = 4

RET_HEADS = 4
RET_QK_DIM = 256
RET_V_DIM = 512
RET_CHUNK = 128
RET_IN = 2 * RET_HEADS * RET_QK_DIM + 2 * RET_HEADS * RET_V_DIM
ATT_Q_HEADS = 16
ATT_KV_HEADS = 4
ATT_GROUP = ATT_Q_HEADS // ATT_KV_HEADS
ATT_HEAD_DIM = 64
WINDOW = 128
ATT_IN = (ATT_Q_HEADS + 2 * ATT_KV_HEADS) * ATT_HEAD_DIM
D_FF = 4 * D_MODEL
N_MIXERS = 2
N_RET_LAYERS = (DEPTH + 1) // 2
N_ATT_LAYERS = DEPTH // 2
EPS = 1e-6
NEG_INF = -1e30

kernel_name = "hybrid_retention_swa_sink_alibi_encoder"


def rms_norm(x, g):
    xf = x.astype(jnp.float32)
    y = xf * lax.rsqrt(jnp.mean(xf * xf, axis=-1, keepdims=True) + EPS)
    return (y * g.astype(jnp.float32)).astype(x.dtype)


def retention_one_direction(q, k, v, log_gamma, strict):
    B, H, S, dk = q.shape
    dv = v.shape[-1]
    C = RET_CHUNK
    n = S // C
    lg = log_gamma.astype(jnp.float32)
    idx = jnp.arange(C, dtype=jnp.float32)
    diff = idx[:, None] - idx[None, :]
    mask = diff > 0 if strict else diff >= 0
    inner_decay = jnp.where(mask[None], jnp.exp(lg[:, None, None] * jnp.where(mask, diff, 0.0)[None]), 0.0)
    q_decay = jnp.exp(lg[:, None] * (idx + 1.0)[None])
    k_decay = jnp.exp(lg[:, None] * (C - 1.0 - idx)[None])
    chunk_decay = jnp.exp(lg * C)

    def to_chunks(a):
        return a.reshape(B, H, n, C, a.shape[-1]).transpose(2, 0, 1, 3, 4)

    def step(R, inp):
        qi, ki, vi = inp
        s = jnp.einsum('bhid,bhjd->bhij', qi, ki) * inner_decay
        inner = jnp.einsum('bhij,bhjv->bhiv', s, vi)
        cross = jnp.einsum('bhid,bhdv->bhiv', qi * q_decay[..., None], R)
        R_new = R * chunk_decay[:, None, None] + jnp.einsum('bhjd,bhjv->bhdv', ki * k_decay[..., None], vi)
        return R_new, inner + cross

    R0 = jnp.zeros((B, H, dk, dv), jnp.float32)
    _, out = lax.scan(step, R0, (to_chunks(q), to_chunks(k), to_chunks(v)))
    return out.transpose(1, 2, 0, 3, 4).reshape(B, H, S, dv)


def retention_mixer(h, w_in, w_out, log_decay_fwd, log_decay_bwd):
    B, S, _ = h.shape
    proj = h @ w_in
    nq = RET_HEADS * RET_QK_DIM
    nv = RET_HEADS * RET_V_DIM
    q, k, v, g = jnp.split(proj, [nq, 2 * nq, 2 * nq + nv], axis=-1)

    def heads(a, d):
        return a.reshape(B, S, RET_HEADS, d).transpose(0, 2, 1, 3).astype(jnp.float32)

    q = heads(q, RET_QK_DIM)
    k = heads(k, RET_QK_DIM) * (RET_QK_DIM ** -0.5)
    v = heads(v, RET_V_DIM)
    fwd = retention_one_direction(q, k, v, log_decay_fwd, strict=False)
    bwd = jnp.flip(retention_one_direction(jnp.flip(q, 2), jnp.flip(k, 2), jnp.flip(v, 2), log_decay_bwd, strict=True), 2)
    o = fwd + bwd
    o = o * lax.rsqrt(jnp.mean(o * o, axis=-1, keepdims=True) + EPS)
    o = o.transpose(0, 2, 1, 3).reshape(B, S, nv).astype(h.dtype)
    return (jax.nn.silu(g) * o) @ w_out


def alibi_slopes(n_heads):
    return jnp.exp2(-8.0 * jnp.arange(1, n_heads + 1, dtype=jnp.float32) / n_heads)


def windowed_gqa_mixer(h, w_in, w_out, sink):
    B, S, _ = h.shape
    W = WINDOW
    nb = S // W
    proj = h @ w_in
    nq = ATT_Q_HEADS * ATT_HEAD_DIM
    nk = ATT_KV_HEADS * ATT_HEAD_DIM
    q, k, v = jnp.split(proj, [nq, nq + nk], axis=-1)
    q = q.reshape(B, nb, W, ATT_KV_HEADS, ATT_GROUP, ATT_HEAD_DIM)

    def band(a):
        a = a.reshape(B, S, ATT_KV_HEADS, ATT_HEAD_DIM)
        a = jnp.pad(a, ((0, 0), (W, W), (0, 0), (0, 0))).reshape(B, nb + 2, W, ATT_KV_HEADS, ATT_HEAD_DIM)
        return jnp.concatenate([a[:, :-2], a[:, 1:-1], a[:, 2:]], axis=2)

    kb, vb = band(k), band(v)
    scores = jnp.einsum('bnqhgd,bnkhd->bnhgqk', q, kb).astype(jnp.float32) * (ATT_HEAD_DIM ** -0.5)
    i = jnp.arange(W)[:, None]
    j = jnp.arange(3 * W)[None, :]
    dist = jnp.abs(W + i - j)
    kpos = (jnp.arange(nb)[:, None] - 1) * W + jnp.arange(3 * W)[None, :]
    valid = (dist <= W)[None] & ((kpos >= 0) & (kpos < S))[:, None, :]
    slopes = alibi_slopes(ATT_Q_HEADS).reshape(ATT_KV_HEADS, ATT_GROUP)
    bias = -slopes[:, :, None, None] * dist.astype(jnp.float32)[None, None]
    scores = jnp.where(valid[None, :, None, None], scores + bias[None, None], NEG_INF)
    sink_col = jnp.broadcast_to(sink.astype(jnp.float32).reshape(ATT_KV_HEADS, ATT_GROUP)[None, None, :, :, None, None],
                                scores.shape[:-1] + (1,))
    probs = jax.nn.softmax(jnp.concatenate([scores, sink_col], axis=-1), axis=-1)[..., :-1]
    out = jnp.einsum('bnhgqk,bnkhd->bnqhgd', probs.astype(vb.dtype), vb)
    return out.reshape(B, S, nq) @ w_out


def squared_relu_mlp(h, w_up, w_down):
    a = jax.nn.relu(h @ w_up)
    return (a * a) @ w_down


def setup_inputs(seed: int = 0) -> dict:
    key = jax.random.key(seed)
    ks = jax.random.split(key, 16)
    f32 = jnp.float32

    def w(k, shape, fan_in):
        return jax.random.normal(k, shape, f32) * (fan_in ** -0.5)

    def gain(k):
        return 1.0 + 0.05 * jax.random.normal(k, (DEPTH, D_MODEL), f32)

    base = jnp.log1p(-jnp.exp2(-5.0 - jnp.arange(RET_HEADS, dtype=f32)))
    return {
        "x": jax.random.normal(ks[0], (BATCH, SEQ, D_MODEL), f32),
        "w_in_ret": w(ks[1], (N_RET_LAYERS, D_MODEL, RET_IN), D_MODEL),
        "w_out_ret": w(ks[2], (N_RET_LAYERS, RET_HEADS * RET_V_DIM, D_MODEL), RET_HEADS * RET_V_DIM),
        "log_decay_fwd": base[None] * (1.0 + 0.05 * jax.random.normal(ks[3], (N_RET_LAYERS, RET_HEADS), f32)),
        "log_decay_bwd": base[None] * (1.0 + 0.05 * jax.random.normal(ks[4], (N_RET_LAYERS, RET_HEADS), f32)),
        "w_in_attn": w(ks[5], (N_ATT_LAYERS, D_MODEL, ATT_IN), D_MODEL),
        "w_out_attn": w(ks[6], (N_ATT_LAYERS, ATT_Q_HEADS * ATT_HEAD_DIM, D_MODEL), ATT_Q_HEADS * ATT_HEAD_DIM),
        "sink_logits": 0.5 * jax.random.normal(ks[7], (N_ATT_LAYERS, ATT_Q_HEADS), f32),
        "g_pre_mix": gain(ks[8]),
        "g_post_mix": gain(ks[9]),
        "g_pre_mlp": gain(ks[10]),
        "g_post_mlp": gain(ks[11]),
        "w_up": w(ks[12], (DEPTH, D_MODEL, D_FF), D_MODEL),
        "w_down": w(ks[13], (DEPTH, D_FF, D_MODEL), D_FF),
    }


def reference(x, w_in_ret, w_out_ret, log_decay_fwd, log_decay_bwd, w_in_attn, w_out_attn,
              sink_logits, g_pre_mix, g_post_mix, g_pre_mlp, g_post_mlp, w_up, w_down):
    for layer in range(DEPTH):
        h = rms_norm(x, g_pre_mix[layer])
        if layer % N_MIXERS == 0:
            r = layer // N_MIXERS
            h = retention_mixer(h, w_in_ret[r], w_out_ret[r], log_decay_fwd[r], log_decay_bwd[r])
        else:
            a = layer // N_MIXERS
            h = windowed_gqa_mixer(h, w_in_attn[a], w_out_attn[a], sink_logits[a])
        x = x + rms_norm(h, g_post_mix[layer])
        h = squared_relu_mlp(rms_norm(x, g_pre_mlp[layer]), w_up[layer], w_down[layer])
        x = x + rms_norm(h, g_post_mlp[layer])
    return x
```

```python
import functools

import jax
import jax.numpy as jnp
from jax import lax
from jax.experimental import pallas as pl
from jax.experimental.pallas import tpu as pltpu

D_MODEL = 1024
DEPTH = 4
RET_HEADS = 4
RET_QK_DIM = 256
RET_V_DIM = 512
ATT_Q_HEADS = 16
ATT_KV_HEADS = 4
ATT_GROUP = ATT_Q_HEADS // ATT_KV_HEADS
ATT_HEAD_DIM = 64
WINDOW = 128
D_FF = 4 * D_MODEL
EPS = 1e-6
NEG_INF = -1e30

TOKEN_TILE = 512
RET_CHUNK = 128
VMEM_LIMIT_BYTES = 56 * 1024 * 1024

_BF16 = jnp.bfloat16
_F32 = jnp.float32


def _rms(x, g):
    y = x * lax.rsqrt(jnp.mean(x * x, axis=-1, keepdims=True) + EPS)
    return y * g


def _resident(shape):
    return pl.BlockSpec(shape, lambda *_: (0,) * len(shape), pipeline_mode=pl.Buffered(1))


def _norm_proj_kernel(x_ref, g_ref, w_ref, o_ref):
    h = _rms(x_ref[...], g_ref[...]).astype(_BF16)
    o_ref[...] = jnp.dot(h, w_ref[...], preferred_element_type=_F32).astype(o_ref.dtype)


def _norm_proj(x2d, g, w):
    t, d = x2d.shape
    n = w.shape[1]
    return pl.pallas_call(
        _norm_proj_kernel,
        grid=(t // TOKEN_TILE,),
        in_specs=[
            pl.BlockSpec((TOKEN_TILE, d), lambda i: (i, 0)),
            _resident((1, d)),
            _resident((d, n)),
        ],
        out_specs=pl.BlockSpec((TOKEN_TILE, n), lambda i: (i, 0)),
        out_shape=jax.ShapeDtypeStruct((t, n), _BF16),
        compiler_params=pltpu.CompilerParams(
            dimension_semantics=("arbitrary",), vmem_limit_bytes=VMEM_LIMIT_BYTES),
        name="norm_proj",
    )(x2d, g, w)


def _retention_kernel(lgf_ref, lgb_ref, q_ref, k_ref, v_ref, g_ref, o_ref, cross_ref, state_ref):
    c = RET_CHUNK
    seq = q_ref.shape[1]
    n_chunks = seq // c
    head = pl.program_id(1)
    lf = lgf_ref[head]
    lb = lgb_ref[head]
    k_scale = RET_QK_DIM ** -0.5

    row = lax.broadcasted_iota(jnp.int32, (c, c), 0)
    col = lax.broadcasted_iota(jnp.int32, (c, c), 1)
    diff = (row - col).astype(_F32)
    inner_decay = jnp.where(
        diff >= 0.0,
        jnp.exp(lf * jnp.maximum(diff, 0.0)),
        jnp.exp(lb * jnp.maximum(-diff, 0.0))) * k_scale
    idx = lax.broadcasted_iota(jnp.int32, (c, 1), 0).astype(_F32)
    q_decay_f = jnp.exp(lf * (idx + 1.0))
    k_decay_f = jnp.exp(lf * (c - 1.0 - idx)) * k_scale
    q_decay_b = jnp.exp(lb * (c - idx))
    k_decay_b = jnp.exp(lb * idx) * k_scale
    chunk_decay_f = jnp.exp(lf * c)
    chunk_decay_b = jnp.exp(lb * c)

    def rows(i):
        return pl.ds(pl.multiple_of(i * c, c), c)

    def state_update(k_scaled, v, decay):
        kv = lax.dot_general(k_scaled, v, (((0,), (0,)), ((), ())), preferred_element_type=_F32)
        state_ref[...] = state_ref[...] * decay + kv

    state_ref[...] = jnp.zeros_like(state_ref)

    def bwd_step(t, carry):
        r = rows(n_chunks - 1 - t)
        q = q_ref[0, r, :].astype(_F32)
        k = k_ref[0, r, :].astype(_F32)
        v = v_ref[0, r, :]
        cross_ref[r, :] = jnp.dot((q * q_decay_b).astype(_BF16), state_ref[...].astype(_BF16),
                                  preferred_element_type=_F32)
        state_update((k * k_decay_b).astype(_BF16), v, chunk_decay_b)
        return carry

    lax.fori_loop(0, n_chunks, bwd_step, 0)

    state_ref[...] = jnp.zeros_like(state_ref)

    def fwd_step(i, carry):
        r = rows(i)
        q_b = q_ref[0, r, :]
        k_b = k_ref[0, r, :]
        v = v_ref[0, r, :]
        q = q_b.astype(_F32)
        k = k_b.astype(_F32)
        s = lax.dot_general(q_b, k_b, (((1,), (1,)), ((), ())), preferred_element_type=_F32)
        s = s * inner_decay
        o = jnp.dot(s.astype(_BF16), v, preferred_element_type=_F32)
        o = o + jnp.dot((q * q_decay_f).astype(_BF16), state_ref[...].astype(_BF16),
                        preferred_element_type=_F32)
        o = o + cross_ref[r, :]
        o = o * lax.rsqrt(jnp.mean(o * o, axis=-1, keepdims=True) + EPS)
        gate = g_ref[0, r, :].astype(_F32)
        silu = gate / (1.0 + jnp.exp(-gate))
        o_ref[0, r, :] = (silu * o).astype(o_ref.dtype)
        state_update((k * k_decay_f).astype(_BF16), v, chunk_decay_f)
        return carry

    lax.fori_loop(0, n_chunks, fwd_step, 0)


def _retention_core(proj, log_decay_fwd, log_decay_bwd):
    b, s, _ = proj.shape
    dk, dv, h = RET_QK_DIM, RET_V_DIM, RET_HEADS
    k_off = h * dk // dk
    v_off = 2 * h * dk // dv
    g_off = (2 * h * dk + h * dv) // dv
    smem = pl.BlockSpec(memory_space=pltpu.SMEM)
    return pl.pallas_call(
        _retention_kernel,
        grid=(b, h),
        in_specs=[
            smem, smem,
            pl.BlockSpec((1, s, dk), lambda bi, hi: (bi, 0, hi)),
            pl.BlockSpec((1, s, dk), lambda bi, hi: (bi, 0, k_off + hi)),
            pl.BlockSpec((1, s, dv), lambda bi, hi: (bi, 0, v_off + hi)),
            pl.BlockSpec((1, s, dv), lambda bi, hi: (bi, 0, g_off + hi)),
        ],
        out_specs=pl.BlockSpec((1, s, dv), lambda bi, hi: (bi, 0, hi)),
        out_shape=jax.ShapeDtypeStruct((b, s, h * dv), _BF16),
        scratch_shapes=[
            pltpu.VMEM((s, dv), _F32),
            pltpu.VMEM((dk, dv), _F32),
        ],
        compiler_params=pltpu.CompilerParams(
            dimension_semantics=("arbitrary", "arbitrary"), vmem_limit_bytes=VMEM_LIMIT_BYTES),
        name="retention_core",
    )(log_decay_fwd, log_decay_bwd, proj, proj, proj, proj)


def _attention_kernel(sink_ref, q_ref, k_ref, v_ref, o_ref):
    w = WINDOW
    d = ATT_HEAD_DIM
    nb = k_ref.shape[1] // w
    n = pl.program_id(1)
    prev_blk = jnp.maximum(n - 1, 0)
    next_blk = jnp.minimum(n + 1, nb - 1)

    def rows(i):
        return pl.ds(pl.multiple_of(i * w, w), w)

    qi = lax.broadcasted_iota(jnp.int32, (w, 3 * w), 0)
    kj = lax.broadcasted_iota(jnp.int32, (w, 3 * w), 1)
    dist = jnp.abs(w + qi - kj)
    first_col = jnp.where(n > 0, 0, w)
    end_col = jnp.where(n < nb - 1, 3 * w, 2 * w)
    valid = (dist <= w) & (kj >= first_col) & (kj < end_col)
    dist_f = dist.astype(_F32)
    scale = d ** -0.5

    outs = []
    for h in range(ATT_KV_HEADS):
        ks = slice(h * d, (h + 1) * d)
        k_band = jnp.concatenate(
            [k_ref[0, rows(prev_blk), ks], k_ref[0, rows(n), ks], k_ref[0, rows(next_blk), ks]], axis=0)
        v_band = jnp.concatenate(
            [v_ref[0, rows(prev_blk), ks], v_ref[0, rows(n), ks], v_ref[0, rows(next_blk), ks]], axis=0)
        for g in range(ATT_GROUP):
            head = h * ATT_GROUP + g
            slope = 2.0 ** (-8.0 * (head + 1) / ATT_Q_HEADS)
            q = q_ref[0, :, head * d:(head + 1) * d]
            s = lax.dot_general(q, k_band, (((1,), (1,)), ((), ())), preferred_element_type=_F32)
            s = jnp.where(valid, s * scale - slope * dist_f, NEG_INF)
            sink = sink_ref[head]
            m = jnp.maximum(jnp.max(s, axis=-1, keepdims=True), sink)
            p = jnp.exp(s - m)
            denom = jnp.sum(p, axis=-1, keepdims=True) + jnp.exp(sink - m)
            pv = jnp.dot(p.astype(_BF16), v_band, preferred_element_type=_F32)
            outs.append(pv / denom)
    o_ref[0] = jnp.concatenate(outs, axis=-1).astype(o_ref.dtype)


def _attention_core(proj, sink):
    b, s, _ = proj.shape
    nq = ATT_Q_HEADS * ATT_HEAD_DIM
    nk = ATT_KV_HEADS * ATT_HEAD_DIM
    return pl.pallas_call(
        _attention_kernel,
        grid=(b, s // WINDOW),
        in_specs=[
            pl.BlockSpec(memory_space=pltpu.SMEM),
            pl.BlockSpec((1, WINDOW, nq), lambda bi, ni: (bi, ni, 0)),
            pl.BlockSpec((1, s, nk), lambda bi, ni: (bi, 0, nq // nk)),
            pl.BlockSpec((1, s, nk), lambda bi, ni: (bi, 0, nq // nk + 1)),
        ],
        out_specs=pl.BlockSpec((1, WINDOW, nq), lambda bi, ni: (bi, ni, 0)),
        out_shape=jax.ShapeDtypeStruct((b, s, nq), _BF16),
        compiler_params=pltpu.CompilerParams(
            dimension_semantics=("arbitrary", "arbitrary"), vmem_limit_bytes=VMEM_LIMIT_BYTES),
        name="attention_core",
    )(sink, proj, proj, proj)


def _out_mlp_kernel(m_ref, x_ref, w_out_ref, g_post_mix_ref, g_pre_mlp_ref, w_up_ref, w_down_ref,
                    g_post_mlp_ref, o_ref):
    y = jnp.dot(m_ref[...], w_out_ref[...], preferred_element_type=_F32)
    x = x_ref[...] + _rms(y, g_post_mix_ref[...])
    h = _rms(x, g_pre_mlp_ref[...]).astype(_BF16)
    a = jnp.maximum(jnp.dot(h, w_up_ref[...], preferred_element_type=_F32), 0.0)
    a = (a * a).astype(_BF16)
    y = jnp.dot(a, w_down_ref[...], preferred_element_type=_F32)
    o_ref[...] = x + _rms(y, g_post_mlp_ref[...])


def _out_mlp(mixed, x2d, w_out, g_post_mix, g_pre_mlp, w_up, w_down, g_post_mlp):
    t, d = x2d.shape
    dm = mixed.shape[1]
    return pl.pallas_call(
        _out_mlp_kernel,
        grid=(t // TOKEN_TILE,),
        in_specs=[
            pl.BlockSpec((TOKEN_TILE, dm), lambda i: (i, 0)),
            pl.BlockSpec((TOKEN_TILE, d), lambda i: (i, 0)),
            _resident((dm, d)),
            _resident((1, d)),
            _resident((1, d)),
            _resident((d, D_FF)),
            _resident((D_FF, d)),
            _resident((1, d)),
        ],
        out_specs=pl.BlockSpec((TOKEN_TILE, d), lambda i: (i, 0)),
        out_shape=jax.ShapeDtypeStruct((t, d), _F32),
        compiler_params=pltpu.CompilerParams(
            dimension_semantics=("arbitrary",), vmem_limit_bytes=VMEM_LIMIT_BYTES),
        name="out_mlp",
    )(mixed, x2d, w_out, g_post_mix, g_pre_mlp, w_up, w_down, g_post_mlp)


def kernel(x, w_in_ret, w_out_ret, log_decay_fwd, log_decay_bwd, w_in_attn, w_out_attn, sink_logits,
           g_pre_mix, g_post_mix, g_pre_mlp, g_post_mlp, w_up, w_down):
    b, s, d = x.shape
    x2d = x.reshape(b * s, d)
    for layer in range(DEPTH):
        gain = lambda g: g[layer].reshape(1, d)
        if layer % 2 == 0:
            r = layer // 2
            proj = _norm_proj(x2d, gain(g_pre_mix), w_in_ret[r].astype(_BF16))
            mixed = _retention_core(proj.reshape(b, s, -1), log_decay_fwd[r], log_decay_bwd[r])
            w_out = w_out_ret[r]
        else:
            a = layer // 2
            proj = _norm_proj(x2d, gain(g_pre_mix), w_in_attn[a].astype(_BF16))
            mixed = _attention_core(proj.reshape(b, s, -1), sink_logits[a])
            w_out = w_out_attn[a]
        x2d = _out_mlp(mixed.reshape(b * s, -1), x2d, w_out.astype(_BF16), gain(g_post_mix),
                       gain(g_pre_mlp), w_up[layer].astype(_BF16), w_down[layer].astype(_BF16),
                       gain(g_post_mlp))
    return x2d.reshape(b, s, d)
```

```python
import jax
import jax.numpy as jnp
from jax import lax
from jax.experimental import pallas as pl
from jax.experimental.pallas import tpu as pltpu

D_MODEL = 1024
DEPTH = 4
RET_HEADS = 4
RET_QK_DIM = 256
RET_V_DIM = 512
ATT_Q_HEADS = 16
ATT_KV_HEADS = 4
ATT_GROUP = ATT_Q_HEADS // ATT_KV_HEADS
ATT_HEAD_DIM = 64
WINDOW = 128
D_FF = 4 * D_MODEL
EPS = 1e-6
NEG_INF = -1e30

TOKEN_TILE = 512
RET_CHUNK = 256
VMEM_LIMIT_BYTES = 56 * 1024 * 1024

_BF16 = jnp.bfloat16
_F32 = jnp.float32


def _rms(x, g):
    y = x * lax.rsqrt(jnp.mean(x * x, axis=-1, keepdims=True) + EPS)
    return y * g


def _resident(shape):
    return pl.BlockSpec(shape, lambda *_: (0,) * len(shape), pipeline_mode=pl.Buffered(1))


def _norm_proj_kernel(x_ref, g_ref, w_ref, o_ref):
    h = _rms(x_ref[...], g_ref[...]).astype(_BF16)
    o_ref[...] = jnp.dot(h, w_ref[...], preferred_element_type=_F32).astype(o_ref.dtype)


def _norm_proj(x2d, g, w):
    t, d = x2d.shape
    n = w.shape[1]
    return pl.pallas_call(
        _norm_proj_kernel,
        grid=(t // TOKEN_TILE,),
        in_specs=[
            pl.BlockSpec((TOKEN_TILE, d), lambda i: (i, 0)),
            _resident((1, d)),
            _resident((d, n)),
        ],
        out_specs=pl.BlockSpec((TOKEN_TILE, n), lambda i: (i, 0)),
        out_shape=jax.ShapeDtypeStruct((t, n), _BF16),
        compiler_params=pltpu.CompilerParams(
            dimension_semantics=("arbitrary",), vmem_limit_bytes=VMEM_LIMIT_BYTES),
        name="norm_proj",
    )(x2d, g, w)


def _retention_kernel(lgf_ref, lgb_ref, q_ref, k_ref, v_ref, g_ref, o_ref, rhs_ref, state_ref):
    c = RET_CHUNK
    dk = RET_QK_DIM
    seq = q_ref.shape[1]
    n_chunks = seq // c
    head = pl.program_id(1)
    lf = lgf_ref[head]
    lb = lgb_ref[head]
    k_scale = dk ** -0.5

    row = lax.broadcasted_iota(jnp.int32, (c, c), 0)
    col = lax.broadcasted_iota(jnp.int32, (c, c), 1)
    diff = (row - col).astype(_F32)
    inner_decay = jnp.where(
        diff >= 0.0,
        jnp.exp(lf * jnp.maximum(diff, 0.0)),
        jnp.exp(lb * jnp.maximum(-diff, 0.0))) * k_scale
    idx = lax.broadcasted_iota(jnp.int32, (c, dk), 0).astype(_F32)
    q_decay_f = jnp.exp(lf * (idx + 1.0))
    k_decay_f = jnp.exp(lf * (c - 1.0 - idx)) * k_scale
    q_decay_b = jnp.exp(lb * (c - idx))
    k_decay_b = jnp.exp(lb * idx) * k_scale
    chunk_decay_f = jnp.exp(lf * c)
    chunk_decay_b = jnp.exp(lb * c)

    def rows(i):
        return slice(i * c, (i + 1) * c)

    def kv_product(i, k_decay):
        k = (k_ref[0, rows(i), :].astype(_F32) * k_decay).astype(_BF16)
        return lax.dot_general(k, v_ref[0, rows(i), :], (((0,), (0,)), ((), ())),
                               preferred_element_type=_F32)

    state_ref[...] = jnp.zeros_like(state_ref)
    for i in range(n_chunks):
        rhs_ref[i, 0:c, :] = v_ref[0, rows(i), :]
        rhs_ref[i, c:c + dk, :] = state_ref[...].astype(_BF16)
        if i + 1 < n_chunks:
            state_ref[...] = state_ref[...] * chunk_decay_f + kv_product(i, k_decay_f)
    state_ref[...] = jnp.zeros_like(state_ref)
    for i in reversed(range(n_chunks)):
        rhs_ref[i, c + dk:c + 2 * dk, :] = state_ref[...].astype(_BF16)
        if i > 0:
            state_ref[...] = state_ref[...] * chunk_decay_b + kv_product(i, k_decay_b)

    for i in range(n_chunks):
        q_b = q_ref[0, rows(i), :]
        q = q_b.astype(_F32)
        s = lax.dot_general(q_b, k_ref[0, rows(i), :], (((1,), (1,)), ((), ())),
                            preferred_element_type=_F32)
        lhs = jnp.concatenate(
            [(s * inner_decay).astype(_BF16), (q * q_decay_f).astype(_BF16), (q * q_decay_b).astype(_BF16)],
            axis=1)
        o = jnp.dot(lhs, rhs_ref[i], preferred_element_type=_F32)
        o = o * lax.rsqrt(jnp.mean(o * o, axis=-1, keepdims=True) + EPS)
        gate = g_ref[0, rows(i), :].astype(_F32)
        silu = gate / (1.0 + jnp.exp(-gate))
        o_ref[0, rows(i), :] = (silu * o).astype(o_ref.dtype)


def _retention_core(proj, log_decay_fwd, log_decay_bwd):
    b, s, _ = proj.shape
    dk, dv, h = RET_QK_DIM, RET_V_DIM, RET_HEADS
    k_off = h * dk // dk
    v_off = 2 * h * dk // dv
    g_off = (2 * h * dk + h * dv) // dv
    smem = pl.BlockSpec(memory_space=pltpu.SMEM)
    return pl.pallas_call(
        _retention_kernel,
        grid=(b, h),
        in_specs=[
            smem, smem,
            pl.BlockSpec((1, s, dk), lambda bi, hi: (bi, 0, hi)),
            pl.BlockSpec((1, s, dk), lambda bi, hi: (bi, 0, k_off + hi)),
            pl.BlockSpec((1, s, dv), lambda bi, hi: (bi, 0, v_off + hi)),
            pl.BlockSpec((1, s, dv), lambda bi, hi: (bi, 0, g_off + hi)),
        ],
        out_specs=pl.BlockSpec((1, s, dv), lambda bi, hi: (bi, 0, hi)),
        out_shape=jax.ShapeDtypeStruct((b, s, h * dv), _BF16),
        scratch_shapes=[
            pltpu.VMEM((s // RET_CHUNK, RET_CHUNK + 2 * dk, dv), _BF16),
            pltpu.VMEM((dk, dv), _F32),
        ],
        compiler_params=pltpu.CompilerParams(
            dimension_semantics=("arbitrary", "arbitrary"), vmem_limit_bytes=VMEM_LIMIT_BYTES),
        name="retention_core",
    )(log_decay_fwd, log_decay_bwd, proj, proj, proj, proj)


def _attention_kernel(sink_ref, q_ref, k_ref, v_ref, o_ref, bias_ref):
    w = WINDOW
    d = ATT_HEAD_DIM
    grp = ATT_GROUP
    nb = k_ref.shape[1] // w
    n = pl.program_id(1)
    assert w & (w - 1) == 0
    lane_head = lax.broadcasted_iota(jnp.int32, (1, grp * w), 1) >> (w.bit_length() - 1)

    def per_head(values):
        out = jnp.full((1, grp * w), values[grp - 1], _F32)
        for g in reversed(range(grp - 1)):
            out = jnp.where(lane_head == g, values[g], out)
        return out

    @pl.when((pl.program_id(0) == 0) & (n == 0))
    def _fill_bias():
        key = lax.broadcasted_iota(jnp.int32, (3 * w, grp * w), 0)
        qry = lax.broadcasted_iota(jnp.int32, (3 * w, grp * w), 1) & (w - 1)
        dist = jnp.abs(w + qry - key).astype(_F32)
        for h in range(ATT_KV_HEADS):
            slopes = [2.0 ** (-8.0 * (h * grp + g + 1) / ATT_Q_HEADS) for g in range(grp)]
            bias_ref[h] = -per_head(slopes) * dist

    def rows(i):
        return pl.ds(pl.multiple_of(i * w, w), w)

    prev_rows = rows(jnp.maximum(n - 1, 0))
    cur_rows = rows(n)
    next_rows = rows(jnp.minimum(n + 1, nb - 1))

    key_minus_qry = (lax.broadcasted_iota(jnp.int32, (w, grp * w), 0)
                     - (lax.broadcasted_iota(jnp.int32, (w, grp * w), 1) & (w - 1)))
    never = 2 * w
    prev_valid = key_minus_qry >= jnp.where(n > 0, 0, never)
    next_valid = -key_minus_qry >= jnp.where(n < nb - 1, 0, never)

    pieces = []
    for h in range(ATT_KV_HEADS):
        ks = slice(h * d, (h + 1) * d)
        k_band = jnp.concatenate([k_ref[0, prev_rows, ks], k_ref[0, cur_rows, ks], k_ref[0, next_rows, ks]], axis=0)
        v_band = jnp.concatenate([v_ref[0, prev_rows, ks], v_ref[0, cur_rows, ks], v_ref[0, next_rows, ks]], axis=0)
        heads = [h * grp + g for g in range(grp)]
        q_grp = jnp.concatenate([q_ref[0, :, hd * d:(hd + 1) * d] for hd in heads], axis=0)
        q_grp = q_grp * jnp.asarray(d ** -0.5, _BF16)
        s = lax.dot_general(k_band, q_grp, (((1,), (1,)), ((), ())), preferred_element_type=_F32)
        s = s + bias_ref[h]
        s = jnp.concatenate([
            jnp.where(prev_valid, s[0:w], NEG_INF),
            s[w:2 * w],
            jnp.where(next_valid, s[2 * w:3 * w], NEG_INF)], axis=0)
        sink = per_head([sink_ref[hd] for hd in heads])
        m = jnp.maximum(jnp.max(s, axis=0, keepdims=True), sink)
        p = jnp.exp(s - m)
        denom = jnp.sum(p, axis=0, keepdims=True) + jnp.exp(sink - m)
        pv = lax.dot_general(v_band, p.astype(_BF16), (((0,), (0,)), ((), ())),
                             preferred_element_type=_F32)
        pv = pv / denom
        pieces.extend(pv[:, g * w:(g + 1) * w] for g in range(grp))
    o_ref[0] = jnp.concatenate(pieces, axis=0).T.astype(o_ref.dtype)


def _attention_core(proj, sink):
    b, s, _ = proj.shape
    nq = ATT_Q_HEADS * ATT_HEAD_DIM
    nk = ATT_KV_HEADS * ATT_HEAD_DIM
    return pl.pallas_call(
        _attention_kernel,
        grid=(b, s // WINDOW),
        in_specs=[
            pl.BlockSpec(memory_space=pltpu.SMEM),
            pl.BlockSpec((1, WINDOW, nq), lambda bi, ni: (bi, ni, 0)),
            pl.BlockSpec((1, s, nk), lambda bi, ni: (bi, 0, nq // nk)),
            pl.BlockSpec((1, s, nk), lambda bi, ni: (bi, 0, nq // nk + 1)),
        ],
        out_specs=pl.BlockSpec((1, WINDOW, nq), lambda bi, ni: (bi, ni, 0)),
        out_shape=jax.ShapeDtypeStruct((b, s, nq), _BF16),
        scratch_shapes=[pltpu.VMEM((ATT_KV_HEADS, 3 * WINDOW, ATT_GROUP * WINDOW), _F32)],
        compiler_params=pltpu.CompilerParams(
            dimension_semantics=("arbitrary", "arbitrary"), vmem_limit_bytes=VMEM_LIMIT_BYTES),
        name="attention_core",
    )(sink, proj, proj, proj)


def _out_mlp_kernel(m_ref, x_ref, w_out_ref, g_post_mix_ref, g_pre_mlp_ref, w_up_ref, w_down_ref,
                    g_post_mlp_ref, o_ref):
    y = jnp.dot(m_ref[...], w_out_ref[...], preferred_element_type=_F32)
    x = x_ref[...] + _rms(y, g_post_mix_ref[...])
    h = _rms(x, g_pre_mlp_ref[...]).astype(_BF16)
    a = jnp.maximum(jnp.dot(h, w_up_ref[...], preferred_element_type=_F32), 0.0)
    a = (a * a).astype(_BF16)
    y = jnp.dot(a, w_down_ref[...], preferred_element_type=_F32)
    o_ref[...] = x + _rms(y, g_post_mlp_ref[...])


def _out_mlp(mixed, x2d, w_out, g_post_mix, g_pre_mlp, w_up, w_down, g_post_mlp):
    t, d = x2d.shape
    dm = mixed.shape[1]
    return pl.pallas_call(
        _out_mlp_kernel,
        grid=(t // TOKEN_TILE,),
        in_specs=[
            pl.BlockSpec((TOKEN_TILE, dm), lambda i: (i, 0)),
            pl.BlockSpec((TOKEN_TILE, d), lambda i: (i, 0)),
            _resident((dm, d)),
            _resident((1, d)),
            _resident((1, d)),
            _resident((d, D_FF)),
            _resident((D_FF, d)),
            _resident((1, d)),
        ],
        out_specs=pl.BlockSpec((TOKEN_TILE, d), lambda i: (i, 0)),
        out_shape=jax.ShapeDtypeStruct((t, d), _F32),
        compiler_params=pltpu.CompilerParams(
            dimension_semantics=("arbitrary",), vmem_limit_bytes=VMEM_LIMIT_BYTES),
        name="out_mlp",
    )(mixed, x2d, w_out, g_post_mix, g_pre_mlp, w_up, w_down, g_post_mlp)


def kernel(x, w_in_ret, w_out_ret, log_decay_fwd, log_decay_bwd, w_in_attn, w_out_attn, sink_logits,
           g_pre_mix, g_post_mix, g_pre_mlp, g_post_mlp, w_up, w_down):
    b, s, d = x.shape
    x2d = x.reshape(b * s, d)
    for layer in range(DEPTH):
        gain = lambda g: g[layer].reshape(1, d)
        if layer % 2 == 0:
            r = layer // 2
            proj = _norm_proj(x2d, gain(g_pre_mix), w_in_ret[r].astype(_BF16))
            mixed = _retention_core(proj.reshape(b, s, -1), log_decay_fwd[r], log_decay_bwd[r])
            w_out = w_out_ret[r]
        else:
            a = layer // 2
            proj = _norm_proj(x2d, gain(g_pre_mix), w_in_attn[a].astype(_BF16))
            mixed = _attention_core(proj.reshape(b, s, -1), sink_logits[a])
            w_out = w_out_attn[a]
        x2d = _out_mlp(mixed.reshape(b * s, -1), x2d, w_out.astype(_BF16), gain(g_post_mix),
                       gain(g_pre_mlp), w_up[layer].astype(_BF16), w_down[layer].astype(_BF16),
                       gain(g_post_mlp))
    return x2d.reshape(b, s, d)
```

```python
import jax
import jax.numpy as jnp
from jax import lax
from jax.experimental import pallas as pl
from jax.experimental.pallas import tpu as pltpu

D_MODEL = 1024
DEPTH = 4
RET_HEADS = 4
RET_QK_DIM = 256
RET_V_DIM = 512
ATT_Q_HEADS = 16
ATT_KV_HEADS = 4
ATT_GROUP = ATT_Q_HEADS // ATT_KV_HEADS
ATT_HEAD_DIM = 64
WINDOW = 128
D_FF = 4 * D_MODEL
EPS = 1e-6
NEG_INF = -1e30
LOG2E = 1.4426950408889634

TOKEN_TILE = 512
RET_CHUNK = 256
RET_SCORES_AHEAD = 2
ATT_PAIR = 2
ATT_ONES_PAD = 16
ATT_BLOCKS_PER_STEP = 2
ATT_SCORES_AHEAD = 2
VMEM_LIMIT_BYTES = 56 * 1024 * 1024

_BF16 = jnp.bfloat16
_F32 = jnp.float32


def _rms(x, g):
    y = x * lax.rsqrt(jnp.mean(x * x, axis=-1, keepdims=True) + EPS)
    return y * g


def _resident(shape):
    return pl.BlockSpec(shape, lambda *_: (0,) * len(shape), pipeline_mode=pl.Buffered(1))


def _norm_proj_kernel(x_ref, g_ref, w_ref, o_ref):
    h = _rms(x_ref[...], g_ref[...]).astype(_BF16)
    o_ref[...] = jnp.dot(h, w_ref[...], preferred_element_type=_F32).astype(o_ref.dtype)


def _norm_proj(x2d, g, w):
    t, d = x2d.shape
    n = w.shape[1]
    return pl.pallas_call(
        _norm_proj_kernel,
        grid=(t // TOKEN_TILE,),
        in_specs=[
            pl.BlockSpec((TOKEN_TILE, d), lambda i: (i, 0)),
            _resident((1, d)),
            _resident((d, n)),
        ],
        out_specs=pl.BlockSpec((TOKEN_TILE, n), lambda i: (i, 0)),
        out_shape=jax.ShapeDtypeStruct((t, n), _BF16),
        compiler_params=pltpu.CompilerParams(
            dimension_semantics=("arbitrary",), vmem_limit_bytes=VMEM_LIMIT_BYTES),
        name="norm_proj",
    )(x2d, g, w)


def _retention_kernel(lgf_ref, lgb_ref, q_ref, k_ref, v_ref, g_ref, o_ref, rhs_ref, state_ref):
    c = RET_CHUNK
    dk = RET_QK_DIM
    seq = q_ref.shape[1]
    n_chunks = seq // c
    head = pl.program_id(1)
    lf = lgf_ref[head]
    lb = lgb_ref[head]
    k_scale = dk ** -0.5

    row = lax.broadcasted_iota(jnp.int32, (c, c), 0)
    col = lax.broadcasted_iota(jnp.int32, (c, c), 1)
    diff = (row - col).astype(_F32)
    inner_decay = jnp.where(
        diff >= 0.0,
        jnp.exp(lf * jnp.maximum(diff, 0.0)),
        jnp.exp(lb * jnp.maximum(-diff, 0.0))) * k_scale
    idx = lax.broadcasted_iota(jnp.int32, (c, dk), 0).astype(_F32)
    q_decay_f = jnp.exp(lf * (idx + 1.0)).astype(_BF16)
    k_decay_f = (jnp.exp(lf * (c - 1.0 - idx)) * k_scale).astype(_BF16)
    q_decay_b = jnp.exp(lb * (c - idx)).astype(_BF16)
    k_decay_b = (jnp.exp(lb * idx) * k_scale).astype(_BF16)
    chunk_decay_f = jnp.exp(lf * c)
    chunk_decay_b = jnp.exp(lb * c)

    def rows(i):
        return slice(i * c, (i + 1) * c)

    def kv_product(i, k_decay):
        return lax.dot_general(k_ref[0, rows(i), :] * k_decay, v_ref[0, rows(i), :], (((0,), (0,)), ((), ())),
                               preferred_element_type=_F32)

    state_ref[...] = jnp.zeros_like(state_ref)
    for i in range(n_chunks):
        rhs_ref[i, 0:c, :] = v_ref[0, rows(i), :]
        rhs_ref[i, c:c + dk, :] = state_ref[...].astype(_BF16)
        if i + 1 < n_chunks:
            state_ref[...] = state_ref[...] * chunk_decay_f + kv_product(i, k_decay_f)
    state_ref[...] = jnp.zeros_like(state_ref)
    for i in reversed(range(n_chunks)):
        rhs_ref[i, c + dk:c + 2 * dk, :] = state_ref[...].astype(_BF16)
        if i > 0:
            state_ref[...] = state_ref[...] * chunk_decay_b + kv_product(i, k_decay_b)

    def chunk_scores(i):
        return lax.dot_general(q_ref[0, rows(i), :], k_ref[0, rows(i), :], (((1,), (1,)), ((), ())),
                               preferred_element_type=_F32)

    pending = [chunk_scores(i) for i in range(min(RET_SCORES_AHEAD, n_chunks))]
    for i in range(n_chunks):
        if i + RET_SCORES_AHEAD < n_chunks:
            pending.append(chunk_scores(i + RET_SCORES_AHEAD))
        s = pending.pop(0)
        q = q_ref[0, rows(i), :]
        lhs = jnp.concatenate([(s * inner_decay).astype(_BF16), q * q_decay_f, q * q_decay_b], axis=1)
        o = jnp.dot(lhs, rhs_ref[i], preferred_element_type=_F32)
        o = o * lax.rsqrt(jnp.mean(o * o, axis=-1, keepdims=True) + EPS)
        gate = g_ref[0, rows(i), :].astype(_F32)
        silu = gate / (1.0 + jnp.exp2(gate * -LOG2E))
        o_ref[0, rows(i), :] = (silu * o).astype(o_ref.dtype)


def _retention_core(proj, log_decay_fwd, log_decay_bwd):
    b, s, _ = proj.shape
    dk, dv, h = RET_QK_DIM, RET_V_DIM, RET_HEADS
    k_off = h * dk // dk
    v_off = 2 * h * dk // dv
    g_off = (2 * h * dk + h * dv) // dv
    smem = pl.BlockSpec(memory_space=pltpu.SMEM)
    return pl.pallas_call(
        _retention_kernel,
        grid=(b, h),
        in_specs=[
            smem, smem,
            pl.BlockSpec((1, s, dk), lambda bi, hi: (bi, 0, hi)),
            pl.BlockSpec((1, s, dk), lambda bi, hi: (bi, 0, k_off + hi)),
            pl.BlockSpec((1, s, dv), lambda bi, hi: (bi, 0, v_off + hi)),
            pl.BlockSpec((1, s, dv), lambda bi, hi: (bi, 0, g_off + hi)),
        ],
        out_specs=pl.BlockSpec((1, s, dv), lambda bi, hi: (bi, 0, hi)),
        out_shape=jax.ShapeDtypeStruct((b, s, h * dv), _BF16),
        scratch_shapes=[
            pltpu.VMEM((s // RET_CHUNK, RET_CHUNK + 2 * dk, dv), _BF16),
            pltpu.VMEM((dk, dv), _F32),
        ],
        compiler_params=pltpu.CompilerParams(
            dimension_semantics=("arbitrary", "arbitrary"), vmem_limit_bytes=VMEM_LIMIT_BYTES),
        name="retention_core",
    )(log_decay_fwd, log_decay_bwd, proj, proj, proj, proj)


def _attention_kernel(sink_ref, q_ref, k_ref, v_ref, o_ref, bias_ref):
    w = WINDOW
    d = ATT_HEAD_DIM
    grp = ATT_GROUP
    pair = ATT_PAIR
    n_pairs = ATT_Q_HEADS // pair
    nb = k_ref.shape[1] // w
    step = pl.program_id(1)

    lane_qry = lax.broadcasted_iota(jnp.int32, (1, pair * w), 1)
    second = lane_qry >= w

    @pl.when((pl.program_id(0) == 0) & (step == 0))
    def _fill_bias():
        key = lax.broadcasted_iota(jnp.int32, (3 * w, pair * w), 0)
        qry = lax.broadcasted_iota(jnp.int32, (3 * w, pair * w), 1) & (w - 1)
        dist = jnp.abs(w + qry - key).astype(_F32)
        for i in range(n_pairs):
            slope = [2.0 ** (-8.0 * (pair * i + j + 1) / ATT_Q_HEADS) for j in range(pair)]
            bias_ref[i] = (-jnp.where(second, slope[1], slope[0]) * dist) * LOG2E

    def rows(i):
        return pl.ds(pl.multiple_of(i * w, w), w)

    key_minus_qry = lax.broadcasted_iota(jnp.int32, (w, pair * w), 0) - (lane_qry & (w - 1))
    never = 2 * w
    ones_pad = jnp.ones((w, ATT_ONES_PAD), _BF16)

    class Block:
        def __init__(self, j):
            n = step * ATT_BLOCKS_PER_STEP + j
            self.q_rows = slice(j * w, (j + 1) * w)
            self.key_rows = [rows(jnp.maximum(n - 1, 0)), rows(n), rows(jnp.minimum(n + 1, nb - 1))]
            self.prev_bound = jnp.where(n > 0, 0, never)
            self.next_bound = jnp.where(n < nb - 1, 0, never)
            self.v_ones = [[jnp.concatenate([v_ref[0, r, h * d:(h + 1) * d], ones_pad], axis=1)
                            for r in self.key_rows] for h in range(ATT_KV_HEADS)]

    def scores(blk, i):
        h = i * pair // grp
        ks = slice(h * d, (h + 1) * d)
        k_band = jnp.concatenate([k_ref[0, r, ks] for r in blk.key_rows], axis=0)
        q_pair = jnp.concatenate([q_ref[0, blk.q_rows, hd * d:(hd + 1) * d]
                                  for hd in range(pair * i, pair * (i + 1))], axis=0)
        q_pair = (q_pair.astype(_F32) * (d ** -0.5 * LOG2E)).astype(_BF16)
        return lax.dot_general(k_band, q_pair, (((1,), (1,)), ((), ())), preferred_element_type=_F32)

    def attend(blk, i, s):
        h = i * pair // grp
        sink = jnp.where(second, sink_ref[pair * i + 1], sink_ref[pair * i]) * LOG2E
        maxes, accs = [], []
        for c in range(3):
            t = s[c * w:(c + 1) * w] + bias_ref[i, c * w:(c + 1) * w, :]
            if c == 0:
                t = jnp.where(key_minus_qry >= blk.prev_bound, t, NEG_INF * LOG2E)
            if c == 2:
                t = jnp.where(-key_minus_qry >= blk.next_bound, t, NEG_INF * LOG2E)
            m = jnp.maximum(jnp.max(t, axis=0, keepdims=True), sink)
            p = jnp.exp2(t - m)
            maxes.append(m)
            accs.append(lax.dot_general(blk.v_ones[h][c], p.astype(_BF16), (((0,), (0,)), ((), ())),
                                        preferred_element_type=_F32))
        m_all = jnp.maximum(jnp.maximum(maxes[0], maxes[1]), maxes[2])
        weights = [jnp.exp2(m - m_all) for m in maxes]
        total = accs[0] * weights[0] + accs[1] * weights[1] + accs[2] * weights[2]
        return total[0:d] / (total[d:d + 1] + jnp.exp2(sink - m_all))

    blocks = [Block(j) for j in range(ATT_BLOCKS_PER_STEP)]
    work = [(blk, i) for blk in blocks for i in range(n_pairs)]
    pending = [scores(*item) for item in work[:ATT_SCORES_AHEAD]]
    pieces = []
    for k, (blk, i) in enumerate(work):
        if k + ATT_SCORES_AHEAD < len(work):
            pending.append(scores(*work[k + ATT_SCORES_AHEAD]))
        out = attend(blk, i, pending.pop(0))
        pieces.extend(out[:, j * w:(j + 1) * w] for j in range(pair))
        if i == n_pairs - 1:
            o_ref[0, blk.q_rows, :] = jnp.concatenate(pieces, axis=0).T.astype(o_ref.dtype)
            pieces = []


def _attention_core(proj, sink):
    b, s, _ = proj.shape
    nq = ATT_Q_HEADS * ATT_HEAD_DIM
    nk = ATT_KV_HEADS * ATT_HEAD_DIM
    q_tile = ATT_BLOCKS_PER_STEP * WINDOW
    return pl.pallas_call(
        _attention_kernel,
        grid=(b, s // q_tile),
        in_specs=[
            pl.BlockSpec(memory_space=pltpu.SMEM),
            pl.BlockSpec((1, q_tile, nq), lambda bi, ni: (bi, ni, 0)),
            pl.BlockSpec((1, s, nk), lambda bi, ni: (bi, 0, nq // nk)),
            pl.BlockSpec((1, s, nk), lambda bi, ni: (bi, 0, nq // nk + 1)),
        ],
        out_specs=pl.BlockSpec((1, q_tile, nq), lambda bi, ni: (bi, ni, 0)),
        out_shape=jax.ShapeDtypeStruct((b, s, nq), _BF16),
        scratch_shapes=[pltpu.VMEM((ATT_Q_HEADS // ATT_PAIR, 3 * WINDOW, ATT_PAIR * WINDOW), _F32)],
        compiler_params=pltpu.CompilerParams(
            dimension_semantics=("arbitrary", "arbitrary"), vmem_limit_bytes=VMEM_LIMIT_BYTES),
        name="attention_core",
    )(sink, proj, proj, proj)


def _out_mlp_kernel(m_ref, x_ref, w_out_ref, g_post_mix_ref, g_pre_mlp_ref, w_up_ref, w_down_ref,
                    g_post_mlp_ref, o_ref):
    y = jnp.dot(m_ref[...], w_out_ref[...], preferred_element_type=_F32)
    x = x_ref[...] + _rms(y, g_post_mix_ref[...])
    h = _rms(x, g_pre_mlp_ref[...]).astype(_BF16)
    a = jnp.maximum(jnp.dot(h, w_up_ref[...], preferred_element_type=_F32), 0.0)
    a = (a * a).astype(_BF16)
    y = jnp.dot(a, w_down_ref[...], preferred_element_type=_F32)
    o_ref[...] = x + _rms(y, g_post_mlp_ref[...])


def _out_mlp(mixed, x2d, w_out, g_post_mix, g_pre_mlp, w_up, w_down, g_post_mlp):
    t, d = x2d.shape
    dm = mixed.shape[1]
    return pl.pallas_call(
        _out_mlp_kernel,
        grid=(t // TOKEN_TILE,),
        in_specs=[
            pl.BlockSpec((TOKEN_TILE, dm), lambda i: (i, 0)),
            pl.BlockSpec((TOKEN_TILE, d), lambda i: (i, 0)),
            _resident((dm, d)),
            _resident((1, d)),
            _resident((1, d)),
            _resident((d, D_FF)),
            _resident((D_FF, d)),
            _resident((1, d)),
        ],
        out_specs=pl.BlockSpec((TOKEN_TILE, d), lambda i: (i, 0)),
        out_shape=jax.ShapeDtypeStruct((t, d), _F32),
        compiler_params=pltpu.CompilerParams(
            dimension_semantics=("arbitrary",), vmem_limit_bytes=VMEM_LIMIT_BYTES),
        name="out_mlp",
    )(mixed, x2d, w_out, g_post_mix, g_pre_mlp, w_up, w_down, g_post_mlp)


def kernel(x, w_in_ret, w_out_ret, log_decay_fwd, log_decay_bwd, w_in_attn, w_out_attn, sink_logits,
           g_pre_mix, g_post_mix, g_pre_mlp, g_post_mlp, w_up, w_down):
    b, s, d = x.shape
    x2d = x.reshape(b * s, d)
    for layer in range(DEPTH):
        gain = lambda g: g[layer].reshape(1, d)
        if layer % 2 == 0:
            r = layer // 2
            proj = _norm_proj(x2d, gain(g_pre_mix), w_in_ret[r].astype(_BF16))
            mixed = _retention_core(proj.reshape(b, s, -1), log_decay_fwd[r], log_decay_bwd[r])
            w_out = w_out_ret[r]
        else:
            a = layer // 2
            proj = _norm_proj(x2d, gain(g_pre_mix), w_in_attn[a].astype(_BF16))
            mixed = _attention_core(proj.reshape(b, s, -1), sink_logits[a])
            w_out = w_out_attn[a]
        x2d = _out_mlp(mixed.reshape(b * s, -1), x2d, w_out.astype(_BF16), gain(g_post_mix),
                       gain(g_pre_mlp), w_up[layer].astype(_BF16), w_down[layer].astype(_BF16),
                       gain(g_post_mlp))
    return x2d.reshape(b, s, d)
```

```python
import jax
import jax.numpy as jnp
from jax import lax
from jax.experimental import pallas as pl
from jax.experimental.pallas import tpu as pltpu

D_MODEL = 1024
DEPTH = 4
RET_HEADS = 4
RET_QK_DIM = 256
RET_V_DIM = 512
ATT_Q_HEADS = 16
ATT_KV_HEADS = 4
ATT_GROUP = ATT_Q_HEADS // ATT_KV_HEADS
ATT_HEAD_DIM = 64
WINDOW = 128
D_FF = 4 * D_MODEL
EPS = 1e-6
NEG_INF = -1e30
LOG2E = 1.4426950408889634

TOKEN_TILE = 1024
PROJ_SUBTILE = 256
MLP_TILE = 512
MLP_SUBTILE = 256
RET_CHUNK = 256
RET_SCORES_AHEAD = 2
ATT_PAIR = 2
ATT_ONES_PAD = 16
ATT_BLOCKS_PER_STEP = 2
ATT_SCORES_AHEAD = 2
VMEM_LIMIT_BYTES = 56 * 1024 * 1024

_BF16 = jnp.bfloat16
_F32 = jnp.float32


def _rms(x, g):
    y = x * lax.rsqrt(jnp.mean(x * x, axis=-1, keepdims=True) + EPS)
    return y * g


def _resident(stacked, index):
    zeros = (0,) * (stacked.ndim - 1)
    return pl.BlockSpec((None,) + stacked.shape[1:], lambda *_: (index,) + zeros, pipeline_mode=pl.Buffered(1))


def _norm_proj_kernel(x_ref, g_ref, w_ref, o_ref):
    for j in range(x_ref.shape[0] // PROJ_SUBTILE):
        r = slice(j * PROJ_SUBTILE, (j + 1) * PROJ_SUBTILE)
        h = _rms(x_ref[r, :], g_ref[...]).astype(_BF16)
        o_ref[r, :] = jnp.dot(h, w_ref[...], preferred_element_type=_F32).astype(o_ref.dtype)


def _norm_proj(x2d, g, layer, w, w_index):
    t, d = x2d.shape
    n = w.shape[2]
    return pl.pallas_call(
        _norm_proj_kernel,
        grid=(t // TOKEN_TILE,),
        in_specs=[
            pl.BlockSpec((TOKEN_TILE, d), lambda i: (i, 0)),
            _resident(g, layer),
            _resident(w, w_index),
        ],
        out_specs=pl.BlockSpec((TOKEN_TILE, n), lambda i: (i, 0)),
        out_shape=jax.ShapeDtypeStruct((t, n), _BF16),
        compiler_params=pltpu.CompilerParams(
            dimension_semantics=("arbitrary",), vmem_limit_bytes=VMEM_LIMIT_BYTES),
        name="norm_proj",
    )(x2d, g, w)


def _retention_kernel(lgf_ref, lgb_ref, q_ref, k_ref, v_ref, g_ref, o_ref, rhs_ref, state_ref):
    c = RET_CHUNK
    dk = RET_QK_DIM
    seq = q_ref.shape[1]
    n_chunks = seq // c
    head = pl.program_id(1)
    lf = lgf_ref[head]
    lb = lgb_ref[head]
    k_scale = dk ** -0.5

    row = lax.broadcasted_iota(jnp.int32, (c, c), 0)
    col = lax.broadcasted_iota(jnp.int32, (c, c), 1)
    diff = (row - col).astype(_F32)
    inner_decay = jnp.where(
        diff >= 0.0,
        jnp.exp(lf * jnp.maximum(diff, 0.0)),
        jnp.exp(lb * jnp.maximum(-diff, 0.0))) * k_scale
    idx = lax.broadcasted_iota(jnp.int32, (c, dk), 0).astype(_F32)
    q_decay_f = jnp.exp(lf * (idx + 1.0)).astype(_BF16)
    k_decay_f = (jnp.exp(lf * (c - 1.0 - idx)) * k_scale).astype(_BF16)
    q_decay_b = jnp.exp(lb * (c - idx)).astype(_BF16)
    k_decay_b = (jnp.exp(lb * idx) * k_scale).astype(_BF16)
    chunk_decay_f = jnp.exp(lf * c)
    chunk_decay_b = jnp.exp(lb * c)

    def rows(i):
        return slice(i * c, (i + 1) * c)

    def kv_product(i, k_decay):
        return lax.dot_general(k_ref[0, rows(i), :] * k_decay, v_ref[0, rows(i), :], (((0,), (0,)), ((), ())),
                               preferred_element_type=_F32)

    state_ref[...] = jnp.zeros_like(state_ref)
    for i in range(n_chunks):
        rhs_ref[i, 0:c, :] = v_ref[0, rows(i), :]
        rhs_ref[i, c:c + dk, :] = state_ref[...].astype(_BF16)
        if i + 1 < n_chunks:
            state_ref[...] = state_ref[...] * chunk_decay_f + kv_product(i, k_decay_f)
    state_ref[...] = jnp.zeros_like(state_ref)
    for i in reversed(range(n_chunks)):
        rhs_ref[i, c + dk:c + 2 * dk, :] = state_ref[...].astype(_BF16)
        if i > 0:
            state_ref[...] = state_ref[...] * chunk_decay_b + kv_product(i, k_decay_b)

    def chunk_scores(i):
        return lax.dot_general(q_ref[0, rows(i), :], k_ref[0, rows(i), :], (((1,), (1,)), ((), ())),
                               preferred_element_type=_F32)

    pending = [chunk_scores(i) for i in range(min(RET_SCORES_AHEAD, n_chunks))]
    for i in range(n_chunks):
        if i + RET_SCORES_AHEAD < n_chunks:
            pending.append(chunk_scores(i + RET_SCORES_AHEAD))
        s = pending.pop(0)
        q = q_ref[0, rows(i), :]
        lhs = jnp.concatenate([(s * inner_decay).astype(_BF16), q * q_decay_f, q * q_decay_b], axis=1)
        o = jnp.dot(lhs, rhs_ref[i], preferred_element_type=_F32)
        o = o * lax.rsqrt(jnp.mean(o * o, axis=-1, keepdims=True) + EPS)
        gate = g_ref[0, rows(i), :].astype(_F32)
        silu = gate / (1.0 + jnp.exp2(gate * -LOG2E))
        o_ref[0, rows(i), :] = (silu * o).astype(o_ref.dtype)


def _retention_core(proj, log_decay_fwd, log_decay_bwd):
    b, s, _ = proj.shape
    dk, dv, h = RET_QK_DIM, RET_V_DIM, RET_HEADS
    k_off = h * dk // dk
    v_off = 2 * h * dk // dv
    g_off = (2 * h * dk + h * dv) // dv
    smem = pl.BlockSpec(memory_space=pltpu.SMEM)
    return pl.pallas_call(
        _retention_kernel,
        grid=(b, h),
        in_specs=[
            smem, smem,
            pl.BlockSpec((1, s, dk), lambda bi, hi: (bi, 0, hi)),
            pl.BlockSpec((1, s, dk), lambda bi, hi: (bi, 0, k_off + hi)),
            pl.BlockSpec((1, s, dv), lambda bi, hi: (bi, 0, v_off + hi)),
            pl.BlockSpec((1, s, dv), lambda bi, hi: (bi, 0, g_off + hi)),
        ],
        out_specs=pl.BlockSpec((1, s, dv), lambda bi, hi: (bi, 0, hi)),
        out_shape=jax.ShapeDtypeStruct((b, s, h * dv), _BF16),
        scratch_shapes=[
            pltpu.VMEM((s // RET_CHUNK, RET_CHUNK + 2 * dk, dv), _BF16),
            pltpu.VMEM((dk, dv), _F32),
        ],
        compiler_params=pltpu.CompilerParams(
            dimension_semantics=("arbitrary", "arbitrary"), vmem_limit_bytes=VMEM_LIMIT_BYTES),
        name="retention_core",
    )(log_decay_fwd, log_decay_bwd, proj, proj, proj, proj)


def _attention_kernel(sink_ref, q_ref, k_ref, v_ref, o_ref, bias_ref):
    w = WINDOW
    d = ATT_HEAD_DIM
    grp = ATT_GROUP
    pair = ATT_PAIR
    n_pairs = ATT_Q_HEADS // pair
    nb = k_ref.shape[1] // w
    step = pl.program_id(1)

    lane_qry = lax.broadcasted_iota(jnp.int32, (1, pair * w), 1)
    second = lane_qry >= w

    @pl.when((pl.program_id(0) == 0) & (step == 0))
    def _fill_bias():
        key = lax.broadcasted_iota(jnp.int32, (3 * w, pair * w), 0)
        qry = lax.broadcasted_iota(jnp.int32, (3 * w, pair * w), 1) & (w - 1)
        dist = jnp.abs(w + qry - key).astype(_F32)
        for i in range(n_pairs):
            slope = [2.0 ** (-8.0 * (pair * i + j + 1) / ATT_Q_HEADS) for j in range(pair)]
            bias_ref[i] = (-jnp.where(second, slope[1], slope[0]) * dist) * LOG2E

    def rows(i):
        return pl.ds(pl.multiple_of(i * w, w), w)

    key_minus_qry = lax.broadcasted_iota(jnp.int32, (w, pair * w), 0) - (lane_qry & (w - 1))
    never = 2 * w
    ones_pad = jnp.ones((w, ATT_ONES_PAD), _BF16)

    class Block:
        def __init__(self, j):
            n = step * ATT_BLOCKS_PER_STEP + j
            self.q_rows = slice(j * w, (j + 1) * w)
            self.key_rows = [rows(jnp.maximum(n - 1, 0)), rows(n), rows(jnp.minimum(n + 1, nb - 1))]
            self.prev_bound = jnp.where(n > 0, 0, never)
            self.next_bound = jnp.where(n < nb - 1, 0, never)
            self.v_ones = [[jnp.concatenate([v_ref[0, r, h * d:(h + 1) * d], ones_pad], axis=1)
                            for r in self.key_rows] for h in range(ATT_KV_HEADS)]

    def scores(blk, i):
        h = i * pair // grp
        ks = slice(h * d, (h + 1) * d)
        k_band = jnp.concatenate([k_ref[0, r, ks] for r in blk.key_rows], axis=0)
        q_pair = jnp.concatenate([q_ref[0, blk.q_rows, hd * d:(hd + 1) * d]
                                  for hd in range(pair * i, pair * (i + 1))], axis=0)
        q_pair = (q_pair.astype(_F32) * (d ** -0.5 * LOG2E)).astype(_BF16)
        return lax.dot_general(k_band, q_pair, (((1,), (1,)), ((), ())), preferred_element_type=_F32)

    def attend(blk, i, s):
        h = i * pair // grp
        sink = jnp.where(second, sink_ref[pair * i + 1], sink_ref[pair * i]) * LOG2E
        maxes, accs = [], []
        for c in range(3):
            t = s[c * w:(c + 1) * w] + bias_ref[i, c * w:(c + 1) * w, :]
            if c == 0:
                t = jnp.where(key_minus_qry >= blk.prev_bound, t, NEG_INF * LOG2E)
            if c == 2:
                t = jnp.where(-key_minus_qry >= blk.next_bound, t, NEG_INF * LOG2E)
            m = jnp.maximum(jnp.max(t, axis=0, keepdims=True), sink)
            p = jnp.exp2(t - m)
            maxes.append(m)
            accs.append(lax.dot_general(blk.v_ones[h][c], p.astype(_BF16), (((0,), (0,)), ((), ())),
                                        preferred_element_type=_F32))
        m_all = jnp.maximum(jnp.maximum(maxes[0], maxes[1]), maxes[2])
        weights = [jnp.exp2(m - m_all) for m in maxes]
        total = accs[0] * weights[0] + accs[1] * weights[1] + accs[2] * weights[2]
        return total[0:d] / (total[d:d + 1] + jnp.exp2(sink - m_all))

    blocks = [Block(j) for j in range(ATT_BLOCKS_PER_STEP)]
    work = [(blk, i) for blk in blocks for i in range(n_pairs)]
    pending = [scores(*item) for item in work[:ATT_SCORES_AHEAD]]
    pieces = []
    for k, (blk, i) in enumerate(work):
        if k + ATT_SCORES_AHEAD < len(work):
            pending.append(scores(*work[k + ATT_SCORES_AHEAD]))
        out = attend(blk, i, pending.pop(0))
        pieces.extend(out[:, j * w:(j + 1) * w] for j in range(pair))
        if i == n_pairs - 1:
            o_ref[0, blk.q_rows, :] = jnp.concatenate(pieces, axis=0).T.astype(o_ref.dtype)
            pieces = []


def _attention_core(proj, sink):
    b, s, _ = proj.shape
    nq = ATT_Q_HEADS * ATT_HEAD_DIM
    nk = ATT_KV_HEADS * ATT_HEAD_DIM
    q_tile = ATT_BLOCKS_PER_STEP * WINDOW
    return pl.pallas_call(
        _attention_kernel,
        grid=(b, s // q_tile),
        in_specs=[
            pl.BlockSpec(memory_space=pltpu.SMEM),
            pl.BlockSpec((1, q_tile, nq), lambda bi, ni: (bi, ni, 0)),
            pl.BlockSpec((1, s, nk), lambda bi, ni: (bi, 0, nq // nk)),
            pl.BlockSpec((1, s, nk), lambda bi, ni: (bi, 0, nq // nk + 1)),
        ],
        out_specs=pl.BlockSpec((1, q_tile, nq), lambda bi, ni: (bi, ni, 0)),
        out_shape=jax.ShapeDtypeStruct((b, s, nq), _BF16),
        scratch_shapes=[pltpu.VMEM((ATT_Q_HEADS // ATT_PAIR, 3 * WINDOW, ATT_PAIR * WINDOW), _F32)],
        compiler_params=pltpu.CompilerParams(
            dimension_semantics=("arbitrary", "arbitrary"), vmem_limit_bytes=VMEM_LIMIT_BYTES),
        name="attention_core",
    )(sink, proj, proj, proj)


def _out_mlp_kernel(m_ref, x_ref, w_out_ref, g_post_mix_ref, g_pre_mlp_ref, w_up_ref, w_down_ref,
                    g_post_mlp_ref, o_ref):
    n_sub = m_ref.shape[0] // MLP_SUBTILE
    rows = [slice(j * MLP_SUBTILE, (j + 1) * MLP_SUBTILE) for j in range(n_sub)]
    ys = [jnp.dot(m_ref[r, :], w_out_ref[...], preferred_element_type=_F32) for r in rows]
    xs, acts = [], []
    for r, y in zip(rows, ys):
        x = x_ref[r, :] + _rms(y, g_post_mix_ref[...])
        h = _rms(x, g_pre_mlp_ref[...]).astype(_BF16)
        xs.append(x)
        acts.append(jnp.dot(h, w_up_ref[...], preferred_element_type=_F32))
    zs = []
    for a in acts:
        a = jnp.maximum(a, 0.0)
        zs.append(jnp.dot((a * a).astype(_BF16), w_down_ref[...], preferred_element_type=_F32))
    for r, x, z in zip(rows, xs, zs):
        o_ref[r, :] = x + _rms(z, g_post_mlp_ref[...])


def _out_mlp(mixed, x2d, layer, w_out, w_out_index, g_post_mix, g_pre_mlp, w_up, w_down, g_post_mlp):
    t, d = x2d.shape
    dm = mixed.shape[1]
    return pl.pallas_call(
        _out_mlp_kernel,
        grid=(t // MLP_TILE,),
        in_specs=[
            pl.BlockSpec((MLP_TILE, dm), lambda i: (i, 0)),
            pl.BlockSpec((MLP_TILE, d), lambda i: (i, 0)),
            _resident(w_out, w_out_index),
            _resident(g_post_mix, layer),
            _resident(g_pre_mlp, layer),
            _resident(w_up, layer),
            _resident(w_down, layer),
            _resident(g_post_mlp, layer),
        ],
        out_specs=pl.BlockSpec((MLP_TILE, d), lambda i: (i, 0)),
        out_shape=jax.ShapeDtypeStruct((t, d), _F32),
        compiler_params=pltpu.CompilerParams(
            dimension_semantics=("arbitrary",), vmem_limit_bytes=VMEM_LIMIT_BYTES),
        name="out_mlp",
    )(mixed, x2d, w_out, g_post_mix, g_pre_mlp, w_up, w_down, g_post_mlp)


def kernel(x, w_in_ret, w_out_ret, log_decay_fwd, log_decay_bwd, w_in_attn, w_out_attn, sink_logits,
           g_pre_mix, g_post_mix, g_pre_mlp, g_post_mlp, w_up, w_down):
    b, s, d = x.shape
    x2d = x.reshape(b * s, d)
    bf16 = lambda w: w.astype(_BF16)
    w_in_ret, w_out_ret, w_in_attn, w_out_attn, w_up, w_down = map(
        bf16, (w_in_ret, w_out_ret, w_in_attn, w_out_attn, w_up, w_down))
    g_pre_mix, g_post_mix, g_pre_mlp, g_post_mlp = (
        g.reshape(DEPTH, 1, d) for g in (g_pre_mix, g_post_mix, g_pre_mlp, g_post_mlp))
    for layer in range(DEPTH):
        i = layer // 2
        if layer % 2 == 0:
            proj = _norm_proj(x2d, g_pre_mix, layer, w_in_ret, i)
            mixed = _retention_core(proj.reshape(b, s, -1), log_decay_fwd[i], log_decay_bwd[i])
            w_out = w_out_ret
        else:
            proj = _norm_proj(x2d, g_pre_mix, layer, w_in_attn, i)
            mixed = _attention_core(proj.reshape(b, s, -1), sink_logits[i])
            w_out = w_out_attn
        x2d = _out_mlp(mixed.reshape(b * s, -1), x2d, layer, w_out, i, g_post_mix, g_pre_mlp, w_up, w_down,
                       g_post_mlp)
    return x2d.reshape(b, s, d)
```

```python
import jax
import jax.numpy as jnp
from jax import lax
from jax.experimental import pallas as pl
from jax.experimental.pallas import tpu as pltpu

D_MODEL = 1024
DEPTH = 4
RET_HEADS = 4
RET_QK_DIM = 256
RET_V_DIM = 512
ATT_Q_HEADS = 16
ATT_KV_HEADS = 4
ATT_GROUP = ATT_Q_HEADS // ATT_KV_HEADS
ATT_HEAD_DIM = 64
WINDOW = 128
D_FF = 4 * D_MODEL
EPS = 1e-6
NEG_INF = -1e30
LOG2E = 1.4426950408889634

TOKEN_TILE = 1024
PROJ_SUBTILE = 256
MLP_TILE = 512
MLP_SUBTILE = 256
RET_CHUNK = 256
RET_SCORES_AHEAD = 2
ATT_PAIR = 2
ATT_ONES_PAD = 16
ATT_BLOCKS_PER_STEP = 4
ATT_TILE = ATT_BLOCKS_PER_STEP * WINDOW
ATT_SCORES_AHEAD = 2
MLP_UP_CHUNK = 512
MLP_DOWN_CHUNK = 256
VMEM_LIMIT_BYTES = 56 * 1024 * 1024

_BF16 = jnp.bfloat16
_F32 = jnp.float32


def _rms(x, g):
    y = x * lax.rsqrt(jnp.mean(x * x, axis=-1, keepdims=True) + EPS)
    return y * g


def _resident(stacked, index):
    zeros = (0,) * (stacked.ndim - 1)
    return pl.BlockSpec((None,) + stacked.shape[1:], lambda *_: (index,) + zeros, pipeline_mode=pl.Buffered(1))


def _norm_proj_kernel(x_ref, g_ref, w_ref, o_ref):
    for j in range(x_ref.shape[0] // PROJ_SUBTILE):
        r = slice(j * PROJ_SUBTILE, (j + 1) * PROJ_SUBTILE)
        h = _rms(x_ref[r, :], g_ref[...]).astype(_BF16)
        o_ref[r, :] = jnp.dot(h, w_ref[...], preferred_element_type=_F32).astype(o_ref.dtype)


def _norm_proj(x2d, g, layer, w, w_index):
    t, d = x2d.shape
    n = w.shape[2]
    return pl.pallas_call(
        _norm_proj_kernel,
        grid=(t // TOKEN_TILE,),
        in_specs=[
            pl.BlockSpec((TOKEN_TILE, d), lambda i: (i, 0)),
            _resident(g, layer),
            _resident(w, w_index),
        ],
        out_specs=pl.BlockSpec((TOKEN_TILE, n), lambda i: (i, 0)),
        out_shape=jax.ShapeDtypeStruct((t, n), _BF16),
        compiler_params=pltpu.CompilerParams(
            dimension_semantics=("arbitrary",), vmem_limit_bytes=VMEM_LIMIT_BYTES),
        name="norm_proj",
    )(x2d, g, w)


def _retention_kernel(lgf_ref, lgb_ref, q_ref, k_ref, v_ref, g_ref, o_ref, rhs_ref, state_ref):
    c = RET_CHUNK
    dk = RET_QK_DIM
    seq = q_ref.shape[1]
    n_chunks = seq // c
    head = pl.program_id(1)
    lf = lgf_ref[head]
    lb = lgb_ref[head]
    k_scale = dk ** -0.5

    row = lax.broadcasted_iota(jnp.int32, (c, c), 0)
    col = lax.broadcasted_iota(jnp.int32, (c, c), 1)
    diff = (row - col).astype(_F32)
    inner_decay = jnp.where(
        diff >= 0.0,
        jnp.exp(lf * jnp.maximum(diff, 0.0)),
        jnp.exp(lb * jnp.maximum(-diff, 0.0))) * k_scale
    idx = lax.broadcasted_iota(jnp.int32, (c, dk), 0).astype(_F32)
    q_decay_f = jnp.exp(lf * (idx + 1.0)).astype(_BF16)
    k_decay_f = (jnp.exp(lf * (c - 1.0 - idx)) * k_scale).astype(_BF16)
    q_decay_b = jnp.exp(lb * (c - idx)).astype(_BF16)
    k_decay_b = (jnp.exp(lb * idx) * k_scale).astype(_BF16)
    chunk_decay_f = jnp.exp(lf * c)
    chunk_decay_b = jnp.exp(lb * c)

    def rows(i):
        return slice(i * c, (i + 1) * c)

    def kv_product(i, k_decay):
        return lax.dot_general(k_ref[0, rows(i), :] * k_decay, v_ref[0, rows(i), :], (((0,), (0,)), ((), ())),
                               preferred_element_type=_F32)

    state_ref[...] = jnp.zeros_like(state_ref)
    for i in range(n_chunks):
        rhs_ref[i, 0:c, :] = v_ref[0, rows(i), :]
        rhs_ref[i, c:c + dk, :] = state_ref[...].astype(_BF16)
        if i + 1 < n_chunks:
            state_ref[...] = state_ref[...] * chunk_decay_f + kv_product(i, k_decay_f)
    state_ref[...] = jnp.zeros_like(state_ref)
    for i in reversed(range(n_chunks)):
        rhs_ref[i, c + dk:c + 2 * dk, :] = state_ref[...].astype(_BF16)
        if i > 0:
            state_ref[...] = state_ref[...] * chunk_decay_b + kv_product(i, k_decay_b)

    def chunk_scores(i):
        return lax.dot_general(q_ref[0, rows(i), :], k_ref[0, rows(i), :], (((1,), (1,)), ((), ())),
                               preferred_element_type=_F32)

    pending = [chunk_scores(i) for i in range(min(RET_SCORES_AHEAD, n_chunks))]
    for i in range(n_chunks):
        if i + RET_SCORES_AHEAD < n_chunks:
            pending.append(chunk_scores(i + RET_SCORES_AHEAD))
        s = pending.pop(0)
        q = q_ref[0, rows(i), :]
        lhs = jnp.concatenate([(s * inner_decay).astype(_BF16), q * q_decay_f, q * q_decay_b], axis=1)
        o = jnp.dot(lhs, rhs_ref[i], preferred_element_type=_F32)
        o = o * lax.rsqrt(jnp.mean(o * o, axis=-1, keepdims=True) + EPS)
        gate = g_ref[0, rows(i), :].astype(_F32)
        silu = gate / (1.0 + jnp.exp2(gate * -LOG2E))
        o_ref[0, rows(i), :] = (silu * o).astype(o_ref.dtype)


def _retention_core(proj, log_decay_fwd, log_decay_bwd):
    b, s, _ = proj.shape
    dk, dv, h = RET_QK_DIM, RET_V_DIM, RET_HEADS
    k_off = h * dk // dk
    v_off = 2 * h * dk // dv
    g_off = (2 * h * dk + h * dv) // dv
    smem = pl.BlockSpec(memory_space=pltpu.SMEM)
    return pl.pallas_call(
        _retention_kernel,
        grid=(b, h),
        in_specs=[
            smem, smem,
            pl.BlockSpec((1, s, dk), lambda bi, hi: (bi, 0, hi)),
            pl.BlockSpec((1, s, dk), lambda bi, hi: (bi, 0, k_off + hi)),
            pl.BlockSpec((1, s, dv), lambda bi, hi: (bi, 0, v_off + hi)),
            pl.BlockSpec((1, s, dv), lambda bi, hi: (bi, 0, g_off + hi)),
        ],
        out_specs=pl.BlockSpec((1, s, dv), lambda bi, hi: (bi, 0, hi)),
        out_shape=jax.ShapeDtypeStruct((b, s, h * dv), _BF16),
        scratch_shapes=[
            pltpu.VMEM((s // RET_CHUNK, RET_CHUNK + 2 * dk, dv), _BF16),
            pltpu.VMEM((dk, dv), _F32),
        ],
        compiler_params=pltpu.CompilerParams(
            dimension_semantics=("arbitrary", "arbitrary"), vmem_limit_bytes=VMEM_LIMIT_BYTES),
        name="retention_core",
    )(log_decay_fwd, log_decay_bwd, proj, proj, proj, proj)


def _fill_alibi_bias(bias_ref):
    w, pair = WINDOW, ATT_PAIR
    second = lax.broadcasted_iota(jnp.int32, (1, pair * w), 1) >= w
    key = lax.broadcasted_iota(jnp.int32, (3 * w, pair * w), 0)
    qry = lax.broadcasted_iota(jnp.int32, (3 * w, pair * w), 1) & (w - 1)
    dist = jnp.abs(w + qry - key).astype(_F32)
    for i in range(ATT_Q_HEADS // pair):
        slope = [2.0 ** (-8.0 * (pair * i + j + 1) / ATT_Q_HEADS) for j in range(pair)]
        bias_ref[i] = (-jnp.where(second, slope[1], slope[0]) * dist) * LOG2E


def _attention_items(sink_ref, q_ref, k_ref, v_ref, bias_ref, first_block, write_block):
    w = WINDOW
    d = ATT_HEAD_DIM
    grp = ATT_GROUP
    pair = ATT_PAIR
    n_pairs = ATT_Q_HEADS // pair
    nb = k_ref.shape[1] // w

    lane_qry = lax.broadcasted_iota(jnp.int32, (1, pair * w), 1)
    second = lane_qry >= w

    def rows(i):
        return pl.ds(pl.multiple_of(i * w, w), w)

    key_minus_qry = lax.broadcasted_iota(jnp.int32, (w, pair * w), 0) - (lane_qry & (w - 1))
    never = 2 * w
    ones_pad = jnp.ones((w, ATT_ONES_PAD), _BF16)

    class Block:
        def __init__(self, j):
            n = first_block + j
            self.j = j
            self.q_rows = slice(j * w, (j + 1) * w)
            self.key_rows = [rows(jnp.maximum(n - 1, 0)), rows(n), rows(jnp.minimum(n + 1, nb - 1))]
            self.prev_bound = jnp.where(n > 0, 0, never)
            self.next_bound = jnp.where(n < nb - 1, 0, never)
            self.v_ones = [[jnp.concatenate([v_ref[0, r, h * d:(h + 1) * d], ones_pad], axis=1)
                            for r in self.key_rows] for h in range(ATT_KV_HEADS)]

    def scores(blk, i):
        h = i * pair // grp
        ks = slice(h * d, (h + 1) * d)
        k_band = jnp.concatenate([k_ref[0, r, ks] for r in blk.key_rows], axis=0)
        q_pair = jnp.concatenate([q_ref[0, blk.q_rows, hd * d:(hd + 1) * d]
                                  for hd in range(pair * i, pair * (i + 1))], axis=0)
        q_pair = (q_pair.astype(_F32) * (d ** -0.5 * LOG2E)).astype(_BF16)
        return lax.dot_general(k_band, q_pair, (((1,), (1,)), ((), ())), preferred_element_type=_F32)

    def attend(blk, i, s):
        h = i * pair // grp
        sink = jnp.where(second, sink_ref[pair * i + 1], sink_ref[pair * i]) * LOG2E
        maxes, accs = [], []
        for c in range(3):
            t = s[c * w:(c + 1) * w] + bias_ref[i, c * w:(c + 1) * w, :]
            if c == 0:
                t = jnp.where(key_minus_qry >= blk.prev_bound, t, NEG_INF * LOG2E)
            if c == 2:
                t = jnp.where(-key_minus_qry >= blk.next_bound, t, NEG_INF * LOG2E)
            m = jnp.maximum(jnp.max(t, axis=0, keepdims=True), sink)
            p = jnp.exp2(t - m)
            maxes.append(m)
            accs.append(lax.dot_general(blk.v_ones[h][c], p.astype(_BF16), (((0,), (0,)), ((), ())),
                                        preferred_element_type=_F32))
        m_all = jnp.maximum(jnp.maximum(maxes[0], maxes[1]), maxes[2])
        weights = [jnp.exp2(m - m_all) for m in maxes]
        total = accs[0] * weights[0] + accs[1] * weights[1] + accs[2] * weights[2]
        return total[0:d] / (total[d:d + 1] + jnp.exp2(sink - m_all))

    blocks = [Block(j) for j in range(ATT_BLOCKS_PER_STEP)]
    work = [(blk, i) for blk in blocks for i in range(n_pairs)]
    pending = [scores(*item) for item in work[:ATT_SCORES_AHEAD]]
    pieces = []

    def make_item(k):
        def run():
            blk, i = work[k]
            if k + ATT_SCORES_AHEAD < len(work):
                pending.append(scores(*work[k + ATT_SCORES_AHEAD]))
            out = attend(blk, i, pending.pop(0))
            pieces.extend(out[:, j * w:(j + 1) * w] for j in range(pair))
            if i == n_pairs - 1:
                write_block(blk.j, jnp.concatenate(pieces, axis=0).T)
                pieces.clear()
        return run

    return [make_item(k) for k in range(len(work))]


def _out_mlp_kernel(m_ref, x_ref, w_out_ref, g_post_mix_ref, g_pre_mlp_ref, w_up_ref, w_down_ref,
                    g_post_mlp_ref, o_ref):
    n_sub = m_ref.shape[0] // MLP_SUBTILE
    rows = [slice(j * MLP_SUBTILE, (j + 1) * MLP_SUBTILE) for j in range(n_sub)]
    ys = [jnp.dot(m_ref[r, :], w_out_ref[...], preferred_element_type=_F32) for r in rows]
    xs, acts = [], []
    for r, y in zip(rows, ys):
        x = x_ref[r, :] + _rms(y, g_post_mix_ref[...])
        h = _rms(x, g_pre_mlp_ref[...]).astype(_BF16)
        xs.append(x)
        acts.append(jnp.dot(h, w_up_ref[...], preferred_element_type=_F32))
    zs = []
    for a in acts:
        a = jnp.maximum(a, 0.0)
        zs.append(jnp.dot((a * a).astype(_BF16), w_down_ref[...], preferred_element_type=_F32))
    for r, x, z in zip(rows, xs, zs):
        o_ref[r, :] = x + _rms(z, g_post_mlp_ref[...])


def _out_mlp(mixed, x2d, layer, w_out, w_out_index, g_post_mix, g_pre_mlp, w_up, w_down, g_post_mlp):
    t, d = x2d.shape
    dm = mixed.shape[1]
    return pl.pallas_call(
        _out_mlp_kernel,
        grid=(t // MLP_TILE,),
        in_specs=[
            pl.BlockSpec((MLP_TILE, dm), lambda i: (i, 0)),
            pl.BlockSpec((MLP_TILE, d), lambda i: (i, 0)),
            _resident(w_out, w_out_index),
            _resident(g_post_mix, layer),
            _resident(g_pre_mlp, layer),
            _resident(w_up, layer),
            _resident(w_down, layer),
            _resident(g_post_mlp, layer),
        ],
        out_specs=pl.BlockSpec((MLP_TILE, d), lambda i: (i, 0)),
        out_shape=jax.ShapeDtypeStruct((t, d), _F32),
        compiler_params=pltpu.CompilerParams(
            dimension_semantics=("arbitrary",), vmem_limit_bytes=VMEM_LIMIT_BYTES),
        name="out_mlp",
    )(mixed, x2d, w_out, g_post_mix, g_pre_mlp, w_up, w_down, g_post_mlp)


def _attn_mlp_kernel(sink_ref, q_ref, k_ref, v_ref, x_ref, w_out_ref, g_post_mix_ref, g_pre_mlp_ref,
                     w_up_ref, w_down_ref, g_post_mlp_ref, o_ref, bias_ref, mixed_ref):
    t = pl.program_id(0)
    tiles_per_batch = k_ref.shape[1] // ATT_TILE
    n_tiles = pl.num_programs(0) - 1
    tile = jnp.minimum(t, n_tiles - 1)
    slot = t % 2

    @pl.when(t == 0)
    def _init():
        _fill_alibi_bias(bias_ref)
        mixed_ref[1] = jnp.zeros_like(mixed_ref[1])

    def write_block(j, value):
        mixed_ref[slot, j * WINDOW:(j + 1) * WINDOW, :] = value.astype(mixed_ref.dtype)

    items = _attention_items(sink_ref, q_ref, k_ref, v_ref, bias_ref,
                             (tile % tiles_per_batch) * ATT_BLOCKS_PER_STEP, write_block)

    d_ff = w_up_ref.shape[1]
    d = w_down_ref.shape[1]
    n_sub = ATT_TILE // MLP_SUBTILE
    sub_rows = [slice(j * MLP_SUBTILE, (j + 1) * MLP_SUBTILE) for j in range(n_sub)]
    vals = [{} for _ in range(n_sub)]

    def out_proj(j):
        def run():
            vals[j]["y"] = jnp.dot(mixed_ref[1 - slot, sub_rows[j], :], w_out_ref[...], preferred_element_type=_F32)
        return run

    def up(j, c):
        def run():
            v = vals[j]
            if c == 0:
                v["x"] = x_ref[sub_rows[j], :] + _rms(v["y"], g_post_mix_ref[...])
                v["h"] = _rms(v["x"], g_pre_mlp_ref[...]).astype(_BF16)
                v["a"] = []
            cols = slice(c * MLP_UP_CHUNK, (c + 1) * MLP_UP_CHUNK)
            a = jnp.maximum(jnp.dot(v["h"], w_up_ref[:, cols], preferred_element_type=_F32), 0.0)
            v["a"].append((a * a).astype(_BF16))
        return run

    def down(j, c):
        def run():
            v = vals[j]
            if c == 0:
                v["act"] = jnp.concatenate(v["a"], axis=1)
                v["z"] = []
            cols = slice(c * MLP_DOWN_CHUNK, (c + 1) * MLP_DOWN_CHUNK)
            v["z"].append(jnp.dot(v["act"], w_down_ref[:, cols], preferred_element_type=_F32))
        return run

    n_up, n_down = d_ff // MLP_UP_CHUNK, d // MLP_DOWN_CHUNK
    pieces = ([(out_proj(j), w_out_ref.shape[0] * d) for j in range(n_sub)]
              + [(up(j, c), d * MLP_UP_CHUNK) for j in range(n_sub) for c in range(n_up)]
              + [(down(j, c), d_ff * MLP_DOWN_CHUNK) for j in range(n_sub) for c in range(n_down)])
    total, done, acc = sum(weight for _, weight in pieces), 0, 0
    for piece, weight in pieces:
        piece()
        acc += weight
        target = min(len(items), -(-acc * len(items) // total))
        for item in items[done:target]:
            item()
        done = target
    for j in range(n_sub):
        z = jnp.concatenate(vals[j]["z"], axis=1)
        o_ref[sub_rows[j], :] = vals[j]["x"] + _rms(z, g_post_mlp_ref[...])


def _attn_mlp(proj, sink, x2d, layer, w_out, w_out_index, g_post_mix, g_pre_mlp, w_up, w_down, g_post_mlp):
    b, s, _ = proj.shape
    t, d = x2d.shape
    nq = ATT_Q_HEADS * ATT_HEAD_DIM
    nk = ATT_KV_HEADS * ATT_HEAD_DIM
    tiles_per_batch = s // ATT_TILE
    n_tiles = b * tiles_per_batch

    def att_tile(i):
        return jnp.minimum(i, n_tiles - 1)

    def mlp_tile(i):
        return jnp.maximum(i - 1, 0)

    return pl.pallas_call(
        _attn_mlp_kernel,
        grid=(n_tiles + 1,),
        in_specs=[
            pl.BlockSpec(memory_space=pltpu.SMEM),
            pl.BlockSpec((1, ATT_TILE, nq),
                         lambda i: (att_tile(i) // tiles_per_batch, att_tile(i) % tiles_per_batch, 0)),
            pl.BlockSpec((1, s, nk), lambda i: (att_tile(i) // tiles_per_batch, 0, nq // nk)),
            pl.BlockSpec((1, s, nk), lambda i: (att_tile(i) // tiles_per_batch, 0, nq // nk + 1)),
            pl.BlockSpec((ATT_TILE, d), lambda i: (mlp_tile(i), 0)),
            _resident(w_out, w_out_index),
            _resident(g_post_mix, layer),
            _resident(g_pre_mlp, layer),
            _resident(w_up, layer),
            _resident(w_down, layer),
            _resident(g_post_mlp, layer),
        ],
        out_specs=pl.BlockSpec((ATT_TILE, d), lambda i: (mlp_tile(i), 0)),
        out_shape=jax.ShapeDtypeStruct((t, d), _F32),
        scratch_shapes=[
            pltpu.VMEM((ATT_Q_HEADS // ATT_PAIR, 3 * WINDOW, ATT_PAIR * WINDOW), _F32),
            pltpu.VMEM((2, ATT_TILE, nq), _BF16),
        ],
        compiler_params=pltpu.CompilerParams(
            dimension_semantics=("arbitrary",), vmem_limit_bytes=VMEM_LIMIT_BYTES),
        name="attn_mlp",
    )(sink, proj, proj, proj, x2d, w_out, g_post_mix, g_pre_mlp, w_up, w_down, g_post_mlp)


def kernel(x, w_in_ret, w_out_ret, log_decay_fwd, log_decay_bwd, w_in_attn, w_out_attn, sink_logits,
           g_pre_mix, g_post_mix, g_pre_mlp, g_post_mlp, w_up, w_down):
    b, s, d = x.shape
    x2d = x.reshape(b * s, d)
    bf16 = lambda w: w.astype(_BF16)
    w_in_ret, w_out_ret, w_in_attn, w_out_attn, w_up, w_down = map(
        bf16, (w_in_ret, w_out_ret, w_in_attn, w_out_attn, w_up, w_down))
    g_pre_mix, g_post_mix, g_pre_mlp, g_post_mlp = (
        g.reshape(DEPTH, 1, d) for g in (g_pre_mix, g_post_mix, g_pre_mlp, g_post_mlp))
    for layer in range(DEPTH):
        i = layer // 2
        if layer % 2 == 0:
            proj = _norm_proj(x2d, g_pre_mix, layer, w_in_ret, i)
            mixed = _retention_core(proj.reshape(b, s, -1), log_decay_fwd[i], log_decay_bwd[i])
            x2d = _out_mlp(mixed.reshape(b * s, -1), x2d, layer, w_out_ret, i, g_post_mix, g_pre_mlp, w_up,
                           w_down, g_post_mlp)
        else:
            proj = _norm_proj(x2d, g_pre_mix, layer, w_in_attn, i)
            x2d = _attn_mlp(proj.reshape(b, s, -1), sink_logits[i], x2d, layer, w_out_attn, i, g_post_mix,
                            g_pre_mlp, w_up, w_down, g_post_mlp)
    return x2d.reshape(b, s, d)
```

```python
import jax
import jax.numpy as jnp
from jax import lax
from jax.experimental import pallas as pl
from jax.experimental.pallas import tpu as pltpu

D_MODEL = 1024
DEPTH = 4
RET_HEADS = 4
RET_QK_DIM = 256
RET_V_DIM = 512
ATT_Q_HEADS = 16
ATT_KV_HEADS = 4
ATT_GROUP = ATT_Q_HEADS // ATT_KV_HEADS
ATT_HEAD_DIM = 64
WINDOW = 128
D_FF = 4 * D_MODEL
EPS = 1e-6
NEG_INF = -1e30
LOG2E = 1.4426950408889634

TOKEN_TILE = 1024
PROJ_SUBTILE = 256
MLP_TILE = 512
MLP_SUBTILE = 256
RET_CHUNK = 256
RET_SCORES_AHEAD = 2
ATT_PAIR = 2
ATT_ONES_PAD = 16
ATT_BLOCKS_PER_STEP = 4
ATT_TILE = ATT_BLOCKS_PER_STEP * WINDOW
ATT_SCORES_AHEAD = 2
MLP_UP_CHUNK = 512
MLP_DOWN_CHUNK = 256
VMEM_LIMIT_BYTES = 56 * 1024 * 1024

_BF16 = jnp.bfloat16
_F32 = jnp.float32


def _rms(x, g):
    y = x * lax.rsqrt(jnp.mean(x * x, axis=-1, keepdims=True) + EPS)
    return y * g


def _resident(stacked, index):
    zeros = (0,) * (stacked.ndim - 1)
    return pl.BlockSpec((None,) + stacked.shape[1:], lambda *_: (index,) + zeros, pipeline_mode=pl.Buffered(1))


def _convert_specs(jobs, n_steps, step_of):
    in_specs, out_specs, out_shapes = [], [], []
    for w, index in jobs:
        _, k, n = w.shape
        rows = k // n_steps
        assert rows * n_steps == k and rows % 16 == 0, (w.shape, n_steps)
        in_specs.append(pl.BlockSpec((None, rows, n), lambda *g, index=index: (index, step_of(*g), 0)))
        out_specs.append(pl.BlockSpec((None, rows, n), lambda *g: (0, step_of(*g), 0)))
        out_shapes.append(jax.ShapeDtypeStruct((1, k, n), _BF16))
    return in_specs, out_specs, out_shapes


def _with_converts(body, n_in, n_out, n_jobs):
    def kernel(*refs):
        ins, rest = refs[:n_in], refs[n_in:]
        job_ins, rest = rest[:n_jobs], rest[n_jobs:]
        outs, rest = rest[:n_out], rest[n_out:]
        job_outs, scratch = rest[:n_jobs], rest[n_jobs:]
        body(*ins, *outs, *scratch)
        for src, dst in zip(job_ins, job_outs):
            dst[...] = src[...].astype(dst.dtype)
    return kernel


def _norm_proj_kernel(x_ref, g_ref, w_ref, o_ref):
    for j in range(x_ref.shape[0] // PROJ_SUBTILE):
        r = slice(j * PROJ_SUBTILE, (j + 1) * PROJ_SUBTILE)
        h = _rms(x_ref[r, :], g_ref[...]).astype(_BF16)
        o_ref[r, :] = jnp.dot(h, w_ref[...], preferred_element_type=_F32).astype(o_ref.dtype)


def _norm_proj(x2d, g, layer, w, w_index, jobs=()):
    t, d = x2d.shape
    n = w.shape[2]
    n_steps = t // TOKEN_TILE
    job_in, job_out, job_shapes = _convert_specs(jobs, n_steps, lambda i: i)
    out = pl.pallas_call(
        _with_converts(_norm_proj_kernel, 3, 1, len(jobs)),
        grid=(n_steps,),
        in_specs=[
            pl.BlockSpec((TOKEN_TILE, d), lambda i: (i, 0)),
            _resident(g, layer),
            _resident(w, w_index),
        ] + job_in,
        out_specs=[pl.BlockSpec((TOKEN_TILE, n), lambda i: (i, 0))] + job_out,
        out_shape=[jax.ShapeDtypeStruct((t, n), _BF16)] + job_shapes,
        compiler_params=pltpu.CompilerParams(
            dimension_semantics=("arbitrary",), vmem_limit_bytes=VMEM_LIMIT_BYTES),
        name="norm_proj",
    )(x2d, g, w, *(w_f32 for w_f32, _ in jobs))
    return out[0], out[1:]


def _retention_kernel(lgf_ref, lgb_ref, q_ref, k_ref, v_ref, g_ref, o_ref, rhs_ref, state_ref):
    c = RET_CHUNK
    dk = RET_QK_DIM
    seq = q_ref.shape[1]
    n_chunks = seq // c
    head = pl.program_id(1)
    lf = lgf_ref[head]
    lb = lgb_ref[head]
    k_scale = dk ** -0.5

    row = lax.broadcasted_iota(jnp.int32, (c, c), 0)
    col = lax.broadcasted_iota(jnp.int32, (c, c), 1)
    diff = (row - col).astype(_F32)
    inner_decay = jnp.where(
        diff >= 0.0,
        jnp.exp(lf * jnp.maximum(diff, 0.0)),
        jnp.exp(lb * jnp.maximum(-diff, 0.0))) * k_scale
    idx = lax.broadcasted_iota(jnp.int32, (c, dk), 0).astype(_F32)
    q_decay_f = jnp.exp(lf * (idx + 1.0)).astype(_BF16)
    k_decay_f = (jnp.exp(lf * (c - 1.0 - idx)) * k_scale).astype(_BF16)
    q_decay_b = jnp.exp(lb * (c - idx)).astype(_BF16)
    k_decay_b = (jnp.exp(lb * idx) * k_scale).astype(_BF16)
    chunk_decay_f = jnp.exp(lf * c)
    chunk_decay_b = jnp.exp(lb * c)

    def rows(i):
        return slice(i * c, (i + 1) * c)

    def kv_product(i, k_decay):
        return lax.dot_general(k_ref[0, rows(i), :] * k_decay, v_ref[0, rows(i), :], (((0,), (0,)), ((), ())),
                               preferred_element_type=_F32)

    state_ref[...] = jnp.zeros_like(state_ref)
    for i in range(n_chunks):
        rhs_ref[i, 0:c, :] = v_ref[0, rows(i), :]
        rhs_ref[i, c:c + dk, :] = state_ref[...].astype(_BF16)
        if i + 1 < n_chunks:
            state_ref[...] = state_ref[...] * chunk_decay_f + kv_product(i, k_decay_f)
    state_ref[...] = jnp.zeros_like(state_ref)
    for i in reversed(range(n_chunks)):
        rhs_ref[i, c + dk:c + 2 * dk, :] = state_ref[...].astype(_BF16)
        if i > 0:
            state_ref[...] = state_ref[...] * chunk_decay_b + kv_product(i, k_decay_b)

    def chunk_scores(i):
        return lax.dot_general(q_ref[0, rows(i), :], k_ref[0, rows(i), :], (((1,), (1,)), ((), ())),
                               preferred_element_type=_F32)

    pending = [chunk_scores(i) for i in range(min(RET_SCORES_AHEAD, n_chunks))]
    for i in range(n_chunks):
        if i + RET_SCORES_AHEAD < n_chunks:
            pending.append(chunk_scores(i + RET_SCORES_AHEAD))
        s = pending.pop(0)
        q = q_ref[0, rows(i), :]
        lhs = jnp.concatenate([(s * inner_decay).astype(_BF16), q * q_decay_f, q * q_decay_b], axis=1)
        o = jnp.dot(lhs, rhs_ref[i], preferred_element_type=_F32)
        o = o * lax.rsqrt(jnp.mean(o * o, axis=-1, keepdims=True) + EPS)
        gate = g_ref[0, rows(i), :].astype(_F32)
        silu = gate / (1.0 + jnp.exp2(gate * -LOG2E))
        o_ref[0, rows(i), :] = (silu * o).astype(o_ref.dtype)


def _retention_core(proj, log_decay_fwd, log_decay_bwd, jobs=()):
    b, s, _ = proj.shape
    dk, dv, h = RET_QK_DIM, RET_V_DIM, RET_HEADS
    k_off = h * dk // dk
    v_off = 2 * h * dk // dv
    g_off = (2 * h * dk + h * dv) // dv
    smem = pl.BlockSpec(memory_space=pltpu.SMEM)
    job_in, job_out, job_shapes = _convert_specs(jobs, b * h, lambda bi, hi: bi * h + hi)
    out = pl.pallas_call(
        _with_converts(_retention_kernel, 6, 1, len(jobs)),
        grid=(b, h),
        in_specs=[
            smem, smem,
            pl.BlockSpec((1, s, dk), lambda bi, hi: (bi, 0, hi)),
            pl.BlockSpec((1, s, dk), lambda bi, hi: (bi, 0, k_off + hi)),
            pl.BlockSpec((1, s, dv), lambda bi, hi: (bi, 0, v_off + hi)),
            pl.BlockSpec((1, s, dv), lambda bi, hi: (bi, 0, g_off + hi)),
        ] + job_in,
        out_specs=[pl.BlockSpec((1, s, dv), lambda bi, hi: (bi, 0, hi))] + job_out,
        out_shape=[jax.ShapeDtypeStruct((b, s, h * dv), _BF16)] + job_shapes,
        scratch_shapes=[
            pltpu.VMEM((s // RET_CHUNK, RET_CHUNK + 2 * dk, dv), _BF16),
            pltpu.VMEM((dk, dv), _F32),
        ],
        compiler_params=pltpu.CompilerParams(
            dimension_semantics=("arbitrary", "arbitrary"), vmem_limit_bytes=VMEM_LIMIT_BYTES),
        name="retention_core",
    )(log_decay_fwd, log_decay_bwd, proj, proj, proj, proj, *(w_f32 for w_f32, _ in jobs))
    return out[0], out[1:]


def _fill_alibi_bias(bias_ref):
    w, pair = WINDOW, ATT_PAIR
    second = lax.broadcasted_iota(jnp.int32, (1, pair * w), 1) >= w
    key = lax.broadcasted_iota(jnp.int32, (3 * w, pair * w), 0)
    qry = lax.broadcasted_iota(jnp.int32, (3 * w, pair * w), 1) & (w - 1)
    dist = jnp.abs(w + qry - key).astype(_F32)
    for i in range(ATT_Q_HEADS // pair):
        slope = [2.0 ** (-8.0 * (pair * i + j + 1) / ATT_Q_HEADS) for j in range(pair)]
        bias_ref[i] = (-jnp.where(second, slope[1], slope[0]) * dist) * LOG2E


def _attention_items(sink_ref, q_ref, k_ref, v_ref, bias_ref, first_block, write_block):
    w = WINDOW
    d = ATT_HEAD_DIM
    grp = ATT_GROUP
    pair = ATT_PAIR
    n_pairs = ATT_Q_HEADS // pair
    nb = k_ref.shape[1] // w

    lane_qry = lax.broadcasted_iota(jnp.int32, (1, pair * w), 1)
    second = lane_qry >= w

    def rows(i):
        return pl.ds(pl.multiple_of(i * w, w), w)

    key_minus_qry = lax.broadcasted_iota(jnp.int32, (w, pair * w), 0) - (lane_qry & (w - 1))
    never = 2 * w
    ones_pad = jnp.ones((w, ATT_ONES_PAD), _BF16)

    class Block:
        def __init__(self, j):
            n = first_block + j
            self.j = j
            self.q_rows = slice(j * w, (j + 1) * w)
            self.key_rows = [rows(jnp.maximum(n - 1, 0)), rows(n), rows(jnp.minimum(n + 1, nb - 1))]
            self.prev_bound = jnp.where(n > 0, 0, never)
            self.next_bound = jnp.where(n < nb - 1, 0, never)
            self.v_ones = [[jnp.concatenate([v_ref[0, r, h * d:(h + 1) * d], ones_pad], axis=1)
                            for r in self.key_rows] for h in range(ATT_KV_HEADS)]

    def scores(blk, i):
        h = i * pair // grp
        ks = slice(h * d, (h + 1) * d)
        k_band = jnp.concatenate([k_ref[0, r, ks] for r in blk.key_rows], axis=0)
        q_pair = jnp.concatenate([q_ref[0, blk.q_rows, hd * d:(hd + 1) * d]
                                  for hd in range(pair * i, pair * (i + 1))], axis=0)
        q_pair = (q_pair.astype(_F32) * (d ** -0.5 * LOG2E)).astype(_BF16)
        return lax.dot_general(k_band, q_pair, (((1,), (1,)), ((), ())), preferred_element_type=_F32)

    def attend(blk, i, s):
        h = i * pair // grp
        sink = jnp.where(second, sink_ref[pair * i + 1], sink_ref[pair * i]) * LOG2E
        maxes, accs = [], []
        for c in range(3):
            t = s[c * w:(c + 1) * w] + bias_ref[i, c * w:(c + 1) * w, :]
            if c == 0:
                t = jnp.where(key_minus_qry >= blk.prev_bound, t, NEG_INF * LOG2E)
            if c == 2:
                t = jnp.where(-key_minus_qry >= blk.next_bound, t, NEG_INF * LOG2E)
            m = jnp.maximum(jnp.max(t, axis=0, keepdims=True), sink)
            p = jnp.exp2(t - m)
            maxes.append(m)
            accs.append(lax.dot_general(blk.v_ones[h][c], p.astype(_BF16), (((0,), (0,)), ((), ())),
                                        preferred_element_type=_F32))
        m_all = jnp.maximum(jnp.maximum(maxes[0], maxes[1]), maxes[2])
        weights = [jnp.exp2(m - m_all) for m in maxes]
        total = accs[0] * weights[0] + accs[1] * weights[1] + accs[2] * weights[2]
        return total[0:d] / (total[d:d + 1] + jnp.exp2(sink - m_all))

    blocks = [Block(j) for j in range(ATT_BLOCKS_PER_STEP)]
    work = [(blk, i) for blk in blocks for i in range(n_pairs)]
    pending = [scores(*item) for item in work[:ATT_SCORES_AHEAD]]
    pieces = []

    def make_item(k):
        def run():
            blk, i = work[k]
            if k + ATT_SCORES_AHEAD < len(work):
                pending.append(scores(*work[k + ATT_SCORES_AHEAD]))
            out = attend(blk, i, pending.pop(0))
            pieces.extend(out[:, j * w:(j + 1) * w] for j in range(pair))
            if i == n_pairs - 1:
                write_block(blk.j, jnp.concatenate(pieces, axis=0).T)
                pieces.clear()
        return run

    return [make_item(k) for k in range(len(work))]


def _out_mlp_kernel(m_ref, x_ref, w_out_ref, g_post_mix_ref, g_pre_mlp_ref, w_up_ref, w_down_ref,
                    g_post_mlp_ref, o_ref):
    n_sub = m_ref.shape[0] // MLP_SUBTILE
    rows = [slice(j * MLP_SUBTILE, (j + 1) * MLP_SUBTILE) for j in range(n_sub)]
    ys = [jnp.dot(m_ref[r, :], w_out_ref[...], preferred_element_type=_F32) for r in rows]
    xs, acts = [], []
    for r, y in zip(rows, ys):
        x = x_ref[r, :] + _rms(y, g_post_mix_ref[...])
        h = _rms(x, g_pre_mlp_ref[...]).astype(_BF16)
        xs.append(x)
        acts.append(jnp.dot(h, w_up_ref[...], preferred_element_type=_F32))
    zs = []
    for a in acts:
        a = jnp.maximum(a, 0.0)
        zs.append(jnp.dot((a * a).astype(_BF16), w_down_ref[...], preferred_element_type=_F32))
    for r, x, z in zip(rows, xs, zs):
        o_ref[r, :] = x + _rms(z, g_post_mlp_ref[...])


def _out_mlp(mixed, x2d, layer, w_out, g_post_mix, g_pre_mlp, w_up, w_down, g_post_mlp, jobs=()):
    t, d = x2d.shape
    dm = mixed.shape[1]
    n_steps = t // MLP_TILE
    job_in, job_out, job_shapes = _convert_specs(jobs, n_steps, lambda i: i)
    out = pl.pallas_call(
        _with_converts(_out_mlp_kernel, 8, 1, len(jobs)),
        grid=(n_steps,),
        in_specs=[
            pl.BlockSpec((MLP_TILE, dm), lambda i: (i, 0)),
            pl.BlockSpec((MLP_TILE, d), lambda i: (i, 0)),
            _resident(w_out, 0),
            _resident(g_post_mix, layer),
            _resident(g_pre_mlp, layer),
            _resident(w_up, 0),
            _resident(w_down, 0),
            _resident(g_post_mlp, layer),
        ] + job_in,
        out_specs=[pl.BlockSpec((MLP_TILE, d), lambda i: (i, 0))] + job_out,
        out_shape=[jax.ShapeDtypeStruct((t, d), _F32)] + job_shapes,
        compiler_params=pltpu.CompilerParams(
            dimension_semantics=("arbitrary",), vmem_limit_bytes=VMEM_LIMIT_BYTES),
        name="out_mlp",
    )(mixed, x2d, w_out, g_post_mix, g_pre_mlp, w_up, w_down, g_post_mlp, *(w_f32 for w_f32, _ in jobs))
    return out[0], out[1:]


def _attn_mlp_kernel(sink_ref, q_ref, k_ref, v_ref, x_ref, w_out_ref, g_post_mix_ref, g_pre_mlp_ref,
                     w_up_ref, w_down_ref, g_post_mlp_ref, o_ref, bias_ref, mixed_ref):
    t = pl.program_id(0)
    tiles_per_batch = k_ref.shape[1] // ATT_TILE
    n_tiles = pl.num_programs(0) - 1
    tile = jnp.minimum(t, n_tiles - 1)
    slot = t % 2

    @pl.when(t == 0)
    def _init():
        _fill_alibi_bias(bias_ref)
        mixed_ref[1] = jnp.zeros_like(mixed_ref[1])

    def write_block(j, value):
        mixed_ref[slot, j * WINDOW:(j + 1) * WINDOW, :] = value.astype(mixed_ref.dtype)

    items = _attention_items(sink_ref, q_ref, k_ref, v_ref, bias_ref,
                             (tile % tiles_per_batch) * ATT_BLOCKS_PER_STEP, write_block)

    d_ff = w_up_ref.shape[1]
    d = w_down_ref.shape[1]
    n_sub = ATT_TILE // MLP_SUBTILE
    sub_rows = [slice(j * MLP_SUBTILE, (j + 1) * MLP_SUBTILE) for j in range(n_sub)]
    vals = [{} for _ in range(n_sub)]

    def out_proj(j):
        def run():
            vals[j]["y"] = jnp.dot(mixed_ref[1 - slot, sub_rows[j], :], w_out_ref[...], preferred_element_type=_F32)
        return run

    def up(j, c):
        def run():
            v = vals[j]
            if c == 0:
                v["x"] = x_ref[sub_rows[j], :] + _rms(v["y"], g_post_mix_ref[...])
                v["h"] = _rms(v["x"], g_pre_mlp_ref[...]).astype(_BF16)
                v["a"] = []
            cols = slice(c * MLP_UP_CHUNK, (c + 1) * MLP_UP_CHUNK)
            a = jnp.maximum(jnp.dot(v["h"], w_up_ref[:, cols], preferred_element_type=_F32), 0.0)
            v["a"].append((a * a).astype(_BF16))
        return run

    def down(j, c):
        def run():
            v = vals[j]
            if c == 0:
                v["act"] = jnp.concatenate(v["a"], axis=1)
                v["z"] = []
            cols = slice(c * MLP_DOWN_CHUNK, (c + 1) * MLP_DOWN_CHUNK)
            v["z"].append(jnp.dot(v["act"], w_down_ref[:, cols], preferred_element_type=_F32))
        return run

    n_up, n_down = d_ff // MLP_UP_CHUNK, d // MLP_DOWN_CHUNK
    pieces = ([(out_proj(j), w_out_ref.shape[0] * d) for j in range(n_sub)]
              + [(up(j, c), d * MLP_UP_CHUNK) for j in range(n_sub) for c in range(n_up)]
              + [(down(j, c), d_ff * MLP_DOWN_CHUNK) for j in range(n_sub) for c in range(n_down)])
    total, done, acc = sum(weight for _, weight in pieces), 0, 0
    for piece, weight in pieces:
        piece()
        acc += weight
        target = min(len(items), -(-acc * len(items) // total))
        for item in items[done:target]:
            item()
        done = target
    for j in range(n_sub):
        z = jnp.concatenate(vals[j]["z"], axis=1)
        o_ref[sub_rows[j], :] = vals[j]["x"] + _rms(z, g_post_mlp_ref[...])


def _attn_mlp(proj, sink, x2d, layer, w_out, g_post_mix, g_pre_mlp, w_up, w_down, g_post_mlp, jobs=()):
    b, s, _ = proj.shape
    t, d = x2d.shape
    nq = ATT_Q_HEADS * ATT_HEAD_DIM
    nk = ATT_KV_HEADS * ATT_HEAD_DIM
    tiles_per_batch = s // ATT_TILE
    n_tiles = b * tiles_per_batch

    def att_tile(i):
        return jnp.minimum(i, n_tiles - 1)

    def mlp_tile(i):
        return jnp.maximum(i - 1, 0)

    job_in, job_out, job_shapes = _convert_specs(jobs, n_tiles, att_tile)
    out = pl.pallas_call(
        _with_converts(_attn_mlp_kernel, 11, 1, len(jobs)),
        grid=(n_tiles + 1,),
        in_specs=[
            pl.BlockSpec(memory_space=pltpu.SMEM),
            pl.BlockSpec((1, ATT_TILE, nq),
                         lambda i: (att_tile(i) // tiles_per_batch, att_tile(i) % tiles_per_batch, 0)),
            pl.BlockSpec((1, s, nk), lambda i: (att_tile(i) // tiles_per_batch, 0, nq // nk)),
            pl.BlockSpec((1, s, nk), lambda i: (att_tile(i) // tiles_per_batch, 0, nq // nk + 1)),
            pl.BlockSpec((ATT_TILE, d), lambda i: (mlp_tile(i), 0)),
            _resident(w_out, 0),
            _resident(g_post_mix, layer),
            _resident(g_pre_mlp, layer),
            _resident(w_up, 0),
            _resident(w_down, 0),
            _resident(g_post_mlp, layer),
        ] + job_in,
        out_specs=[pl.BlockSpec((ATT_TILE, d), lambda i: (mlp_tile(i), 0))] + job_out,
        out_shape=[jax.ShapeDtypeStruct((t, d), _F32)] + job_shapes,
        scratch_shapes=[
            pltpu.VMEM((ATT_Q_HEADS // ATT_PAIR, 3 * WINDOW, ATT_PAIR * WINDOW), _F32),
            pltpu.VMEM((2, ATT_TILE, nq), _BF16),
        ],
        compiler_params=pltpu.CompilerParams(
            dimension_semantics=("arbitrary",), vmem_limit_bytes=VMEM_LIMIT_BYTES),
        name="attn_mlp",
    )(sink, proj, proj, proj, x2d, w_out, g_post_mix, g_pre_mlp, w_up, w_down, g_post_mlp,
      *(w_f32 for w_f32, _ in jobs))
    return out[0], out[1:]


def kernel(x, w_in_ret, w_out_ret, log_decay_fwd, log_decay_bwd, w_in_attn, w_out_attn, sink_logits,
           g_pre_mix, g_post_mix, g_pre_mlp, g_post_mlp, w_up, w_down):
    b, s, d = x.shape
    x2d = x.reshape(b * s, d)
    g_pre_mix, g_post_mix, g_pre_mlp, g_post_mlp = (
        g.reshape(DEPTH, 1, d) for g in (g_pre_mix, g_post_mix, g_pre_mlp, g_post_mlp))
    w_in = w_in_ret[0:1].astype(_BF16)
    for layer in range(0, DEPTH, 2):
        i = layer // 2
        proj, (w_out, w_up_a, w_down_a) = _norm_proj(
            x2d, g_pre_mix, layer, w_in, 0, jobs=[(w_out_ret, i), (w_up, layer), (w_down, layer)])
        mixed, (w_in_b, w_out_b) = _retention_core(
            proj.reshape(b, s, -1), log_decay_fwd[i], log_decay_bwd[i], jobs=[(w_in_attn, i), (w_out_attn, i)])
        x2d, (w_up_b, w_down_b) = _out_mlp(
            mixed.reshape(b * s, -1), x2d, layer, w_out, g_post_mix, g_pre_mlp, w_up_a, w_down_a, g_post_mlp,
            jobs=[(w_up, layer + 1), (w_down, layer + 1)])
        proj, _ = _norm_proj(x2d, g_pre_mix, layer + 1, w_in_b, 0)
        more = layer + 2 < DEPTH
        x2d, next_w_in = _attn_mlp(
            proj.reshape(b, s, -1), sink_logits[i], x2d, layer + 1, w_out_b, g_post_mix, g_pre_mlp, w_up_b,
            w_down_b, g_post_mlp, jobs=[(w_in_ret, i + 1)] if more else [])
        if more:
            (w_in,) = next_w_in
    return x2d.reshape(b, s, d)
```

```python
import jax
import jax.numpy as jnp
from jax import lax
from jax.experimental import pallas as pl
from jax.experimental.pallas import tpu as pltpu

D_MODEL = 1024
DEPTH = 4
RET_HEADS = 4
RET_QK_DIM = 256
RET_V_DIM = 512
ATT_Q_HEADS = 16
ATT_KV_HEADS = 4
ATT_GROUP = ATT_Q_HEADS // ATT_KV_HEADS
ATT_HEAD_DIM = 64
WINDOW = 128
D_FF = 4 * D_MODEL
EPS = 1e-6
NEG_INF = -1e30
LOG2E = 1.4426950408889634

TOKEN_TILE = 1024
PROJ_SUBTILE = 256
MLP_TILE = 512
MLP_SUBTILE = 256
RET_CHUNK = 256
RET_SCORES_AHEAD = 2
ATT_PAIR = 2
ATT_ONES_PAD = 16
ATT_BLOCKS_PER_STEP = 4
ATT_TILE = ATT_BLOCKS_PER_STEP * WINDOW
ATT_SCORES_AHEAD = 2
MLP_UP_CHUNK = 512
MLP_DOWN_CHUNK = 256
VMEM_LIMIT_BYTES = 56 * 1024 * 1024

_BF16 = jnp.bfloat16
_F32 = jnp.float32


def _rms(x, g):
    y = x * lax.rsqrt(jnp.mean(x * x, axis=-1, keepdims=True) + EPS)
    return y * g


def _resident(stacked, index):
    zeros = (0,) * (stacked.ndim - 1)
    return pl.BlockSpec((None,) + stacked.shape[1:], lambda *_: (index,) + zeros, pipeline_mode=pl.Buffered(1))


def _convert_specs(jobs, n_steps, step_of):
    in_specs, out_specs, out_shapes = [], [], []
    for w, index in jobs:
        _, k, n = w.shape
        rows = k // n_steps
        assert rows * n_steps == k and rows % 16 == 0, (w.shape, n_steps)
        in_specs.append(pl.BlockSpec((None, rows, n), lambda *g, index=index: (index, step_of(*g), 0)))
        out_specs.append(pl.BlockSpec((None, rows, n), lambda *g: (0, step_of(*g), 0)))
        out_shapes.append(jax.ShapeDtypeStruct((1, k, n), _BF16))
    return in_specs, out_specs, out_shapes


def _with_converts(body, n_in, n_out, n_jobs):
    def kernel(*refs):
        ins, rest = refs[:n_in], refs[n_in:]
        job_ins, rest = rest[:n_jobs], rest[n_jobs:]
        outs, rest = rest[:n_out], rest[n_out:]
        job_outs, scratch = rest[:n_jobs], rest[n_jobs:]
        body(*ins, *outs, *scratch)
        for src, dst in zip(job_ins, job_outs):
            dst[...] = src[...].astype(dst.dtype)
    return kernel


def _norm_proj_kernel(x_ref, g_ref, w_ref, o_ref):
    for j in range(x_ref.shape[0] // PROJ_SUBTILE):
        r = slice(j * PROJ_SUBTILE, (j + 1) * PROJ_SUBTILE)
        h = _rms(x_ref[r, :], g_ref[...]).astype(_BF16)
        o_ref[r, :] = jnp.dot(h, w_ref[...], preferred_element_type=_F32).astype(o_ref.dtype)


def _norm_proj(x2d, g, layer, w, w_index, jobs=()):
    t, d = x2d.shape
    n = w.shape[2]
    n_steps = t // TOKEN_TILE
    job_in, job_out, job_shapes = _convert_specs(jobs, n_steps, lambda i: i)
    out = pl.pallas_call(
        _with_converts(_norm_proj_kernel, 3, 1, len(jobs)),
        grid=(n_steps,),
        in_specs=[
            pl.BlockSpec((TOKEN_TILE, d), lambda i: (i, 0)),
            _resident(g, layer),
            _resident(w, w_index),
        ] + job_in,
        out_specs=[pl.BlockSpec((TOKEN_TILE, n), lambda i: (i, 0))] + job_out,
        out_shape=[jax.ShapeDtypeStruct((t, n), _BF16)] + job_shapes,
        compiler_params=pltpu.CompilerParams(
            dimension_semantics=("arbitrary",), vmem_limit_bytes=VMEM_LIMIT_BYTES),
        name="norm_proj",
    )(x2d, g, w, *(w_f32 for w_f32, _ in jobs))
    return out[0], out[1:]


def _retention_kernel(lgf_ref, lgb_ref, q_ref, k_ref, v_ref, g_ref, o_ref, rhs_ref, state_ref):
    c = RET_CHUNK
    dk = RET_QK_DIM
    seq = q_ref.shape[1]
    n_chunks = seq // c
    head = pl.program_id(1)
    lf = lgf_ref[head]
    lb = lgb_ref[head]
    k_scale = dk ** -0.5

    row = lax.broadcasted_iota(jnp.int32, (c, c), 0)
    col = lax.broadcasted_iota(jnp.int32, (c, c), 1)
    diff = (row - col).astype(_F32)
    inner_decay = jnp.where(
        diff >= 0.0,
        jnp.exp(lf * jnp.maximum(diff, 0.0)),
        jnp.exp(lb * jnp.maximum(-diff, 0.0))) * k_scale
    idx = lax.broadcasted_iota(jnp.int32, (c, dk), 0).astype(_F32)
    q_decay_f = jnp.exp(lf * (idx + 1.0)).astype(_BF16)
    k_decay_f = (jnp.exp(lf * (c - 1.0 - idx)) * k_scale).astype(_BF16)
    q_decay_b = jnp.exp(lb * (c - idx)).astype(_BF16)
    k_decay_b = (jnp.exp(lb * idx) * k_scale).astype(_BF16)
    chunk_decay_f = jnp.exp(lf * c)
    chunk_decay_b = jnp.exp(lb * c)

    def rows(i):
        return slice(i * c, (i + 1) * c)

    def kv_product(i, k_decay):
        return lax.dot_general(k_ref[0, rows(i), :] * k_decay, v_ref[0, rows(i), :], (((0,), (0,)), ((), ())),
                               preferred_element_type=_F32)

    state_ref[...] = jnp.zeros_like(state_ref)
    for i in range(n_chunks):
        rhs_ref[i, 0:c, :] = v_ref[0, rows(i), :]
        rhs_ref[i, c:c + dk, :] = state_ref[...].astype(_BF16)
        if i + 1 < n_chunks:
            state_ref[...] = state_ref[...] * chunk_decay_f + kv_product(i, k_decay_f)
    state_ref[...] = jnp.zeros_like(state_ref)
    for i in reversed(range(n_chunks)):
        rhs_ref[i, c + dk:c + 2 * dk, :] = state_ref[...].astype(_BF16)
        if i > 0:
            state_ref[...] = state_ref[...] * chunk_decay_b + kv_product(i, k_decay_b)

    def chunk_scores(i):
        return lax.dot_general(q_ref[0, rows(i), :], k_ref[0, rows(i), :], (((1,), (1,)), ((), ())),
                               preferred_element_type=_F32)

    pending = [chunk_scores(i) for i in range(min(RET_SCORES_AHEAD, n_chunks))]
    for i in range(n_chunks):
        if i + RET_SCORES_AHEAD < n_chunks:
            pending.append(chunk_scores(i + RET_SCORES_AHEAD))
        s = pending.pop(0)
        q = q_ref[0, rows(i), :]
        lhs = jnp.concatenate([(s * inner_decay).astype(_BF16), q * q_decay_f, q * q_decay_b], axis=1)
        o = jnp.dot(lhs, rhs_ref[i], preferred_element_type=_F32)
        o = o * lax.rsqrt(jnp.mean(o * o, axis=-1, keepdims=True) + EPS)
        gate = g_ref[0, rows(i), :].astype(_F32)
        silu = gate / (1.0 + jnp.exp2(gate * -LOG2E))
        o_ref[0, rows(i), :] = (silu * o).astype(o_ref.dtype)


def _retention_core(proj, log_decay_fwd, log_decay_bwd, jobs=()):
    b, s, _ = proj.shape
    dk, dv, h = RET_QK_DIM, RET_V_DIM, RET_HEADS
    k_off = h * dk // dk
    v_off = 2 * h * dk // dv
    g_off = (2 * h * dk + h * dv) // dv
    smem = pl.BlockSpec(memory_space=pltpu.SMEM)
    job_in, job_out, job_shapes = _convert_specs(jobs, b * h, lambda bi, hi: bi * h + hi)
    out = pl.pallas_call(
        _with_converts(_retention_kernel, 6, 1, len(jobs)),
        grid=(b, h),
        in_specs=[
            smem, smem,
            pl.BlockSpec((1, s, dk), lambda bi, hi: (bi, 0, hi)),
            pl.BlockSpec((1, s, dk), lambda bi, hi: (bi, 0, k_off + hi)),
            pl.BlockSpec((1, s, dv), lambda bi, hi: (bi, 0, v_off + hi)),
            pl.BlockSpec((1, s, dv), lambda bi, hi: (bi, 0, g_off + hi)),
        ] + job_in,
        out_specs=[pl.BlockSpec((1, s, dv), lambda bi, hi: (bi, 0, hi))] + job_out,
        out_shape=[jax.ShapeDtypeStruct((b, s, h * dv), _BF16)] + job_shapes,
        scratch_shapes=[
            pltpu.VMEM((s // RET_CHUNK, RET_CHUNK + 2 * dk, dv), _BF16),
            pltpu.VMEM((dk, dv), _F32),
        ],
        compiler_params=pltpu.CompilerParams(
            dimension_semantics=("arbitrary", "arbitrary"), vmem_limit_bytes=VMEM_LIMIT_BYTES),
        name="retention_core",
    )(log_decay_fwd, log_decay_bwd, proj, proj, proj, proj, *(w_f32 for w_f32, _ in jobs))
    return out[0], out[1:]


def _fill_alibi_bias(bias_ref):
    w, pair = WINDOW, ATT_PAIR
    second = lax.broadcasted_iota(jnp.int32, (1, pair * w), 1) >= w
    key = lax.broadcasted_iota(jnp.int32, (3 * w, pair * w), 0)
    qry = lax.broadcasted_iota(jnp.int32, (3 * w, pair * w), 1) & (w - 1)
    dist = jnp.abs(w + qry - key).astype(_F32)
    for i in range(ATT_Q_HEADS // pair):
        slope = [2.0 ** (-8.0 * (pair * i + j + 1) / ATT_Q_HEADS) for j in range(pair)]
        bias_ref[i] = (-jnp.where(second, slope[1], slope[0]) * dist) * LOG2E


def _attention_items(sink_ref, q_ref, k_ref, v_ref, bias_ref, first_block, write_block):
    w = WINDOW
    d = ATT_HEAD_DIM
    grp = ATT_GROUP
    pair = ATT_PAIR
    n_pairs = ATT_Q_HEADS // pair
    nb = k_ref.shape[1] // w

    lane_qry = lax.broadcasted_iota(jnp.int32, (1, pair * w), 1)
    second = lane_qry >= w

    def rows(i):
        return pl.ds(pl.multiple_of(i * w, w), w)

    key_minus_qry = lax.broadcasted_iota(jnp.int32, (w, pair * w), 0) - (lane_qry & (w - 1))
    never = 2 * w
    ones_pad = jnp.ones((w, ATT_ONES_PAD), _BF16)

    class Block:
        def __init__(self, j):
            n = first_block + j
            self.j = j
            self.q_rows = slice(j * w, (j + 1) * w)
            self.key_rows = [rows(jnp.maximum(n - 1, 0)), rows(n), rows(jnp.minimum(n + 1, nb - 1))]
            self.prev_bound = jnp.where(n > 0, 0, never)
            self.next_bound = jnp.where(n < nb - 1, 0, never)
            self.v_ones = [[jnp.concatenate([v_ref[0, r, h * d:(h + 1) * d], ones_pad], axis=1)
                            for r in self.key_rows] for h in range(ATT_KV_HEADS)]

    def scores(blk, i):
        h = i * pair // grp
        ks = slice(h * d, (h + 1) * d)
        k_band = jnp.concatenate([k_ref[0, r, ks] for r in blk.key_rows], axis=0)
        q_pair = jnp.concatenate([q_ref[0, blk.q_rows, hd * d:(hd + 1) * d]
                                  for hd in range(pair * i, pair * (i + 1))], axis=0)
        q_pair = (q_pair.astype(_F32) * (d ** -0.5 * LOG2E)).astype(_BF16)
        return lax.dot_general(k_band, q_pair, (((1,), (1,)), ((), ())), preferred_element_type=_F32)

    def attend(blk, i, s):
        h = i * pair // grp
        sink = jnp.where(second, sink_ref[pair * i + 1], sink_ref[pair * i]) * LOG2E
        maxes, accs = [], []
        for c in range(3):
            t = s[c * w:(c + 1) * w] + bias_ref[i, c * w:(c + 1) * w, :]
            if c == 0:
                t = jnp.where(key_minus_qry >= blk.prev_bound, t, NEG_INF * LOG2E)
            if c == 2:
                t = jnp.where(-key_minus_qry >= blk.next_bound, t, NEG_INF * LOG2E)
            m = jnp.maximum(jnp.max(t, axis=0, keepdims=True), sink)
            p = jnp.exp2(t - m)
            maxes.append(m)
            accs.append(lax.dot_general(blk.v_ones[h][c], p.astype(_BF16), (((0,), (0,)), ((), ())),
                                        preferred_element_type=_F32))
        m_all = jnp.maximum(jnp.maximum(maxes[0], maxes[1]), maxes[2])
        weights = [jnp.exp2(m - m_all) for m in maxes]
        total = accs[0] * weights[0] + accs[1] * weights[1] + accs[2] * weights[2]
        return total[0:d] / (total[d:d + 1] + jnp.exp2(sink - m_all))

    blocks = [Block(j) for j in range(ATT_BLOCKS_PER_STEP)]
    work = [(blk, i) for blk in blocks for i in range(n_pairs)]
    pending = [scores(*item) for item in work[:ATT_SCORES_AHEAD]]
    pieces = []

    def make_item(k):
        def run():
            blk, i = work[k]
            if k + ATT_SCORES_AHEAD < len(work):
                pending.append(scores(*work[k + ATT_SCORES_AHEAD]))
            out = attend(blk, i, pending.pop(0))
            pieces.extend(out[:, j * w:(j + 1) * w] for j in range(pair))
            if i == n_pairs - 1:
                write_block(blk.j, jnp.concatenate(pieces, axis=0).T)
                pieces.clear()
        return run

    return [make_item(k) for k in range(len(work))]


def _out_mlp_kernel(m_ref, x_ref, w_out_ref, g_post_mix_ref, g_pre_mlp_ref, w_up_ref, w_down_ref,
                    g_post_mlp_ref, g_next_ref, w_next_ref, o_ref, p_ref):
    n_sub = m_ref.shape[0] // MLP_SUBTILE
    rows = [slice(j * MLP_SUBTILE, (j + 1) * MLP_SUBTILE) for j in range(n_sub)]
    ys = [jnp.dot(m_ref[r, :], w_out_ref[...], preferred_element_type=_F32) for r in rows]
    xs, acts = [], []
    for r, y in zip(rows, ys):
        x = x_ref[r, :] + _rms(y, g_post_mix_ref[...])
        h = _rms(x, g_pre_mlp_ref[...]).astype(_BF16)
        xs.append(x)
        acts.append(jnp.dot(h, w_up_ref[...], preferred_element_type=_F32))
    zs = []
    for a in acts:
        a = jnp.maximum(a, 0.0)
        zs.append(jnp.dot((a * a).astype(_BF16), w_down_ref[...], preferred_element_type=_F32))
    outs = []
    for r, x, z in zip(rows, xs, zs):
        outs.append(x + _rms(z, g_post_mlp_ref[...]))
        o_ref[r, :] = outs[-1]
    for r, out in zip(rows, outs):
        h = _rms(out, g_next_ref[...]).astype(_BF16)
        p_ref[r, :] = jnp.dot(h, w_next_ref[...], preferred_element_type=_F32).astype(p_ref.dtype)


def _out_mlp(mixed, x2d, layer, w_out, g_post_mix, g_pre_mlp, w_up, w_down, g_post_mlp, g_pre_mix, w_next, jobs=()):
    t, d = x2d.shape
    dm = mixed.shape[1]
    n_next = w_next.shape[2]
    n_steps = t // MLP_TILE
    job_in, job_out, job_shapes = _convert_specs(jobs, n_steps, lambda i: i)
    out = pl.pallas_call(
        _with_converts(_out_mlp_kernel, 10, 2, len(jobs)),
        grid=(n_steps,),
        in_specs=[
            pl.BlockSpec((MLP_TILE, dm), lambda i: (i, 0)),
            pl.BlockSpec((MLP_TILE, d), lambda i: (i, 0)),
            _resident(w_out, 0),
            _resident(g_post_mix, layer),
            _resident(g_pre_mlp, layer),
            _resident(w_up, 0),
            _resident(w_down, 0),
            _resident(g_post_mlp, layer),
            _resident(g_pre_mix, layer + 1),
            _resident(w_next, 0),
        ] + job_in,
        out_specs=[pl.BlockSpec((MLP_TILE, d), lambda i: (i, 0)),
                   pl.BlockSpec((MLP_TILE, n_next), lambda i: (i, 0))] + job_out,
        out_shape=[jax.ShapeDtypeStruct((t, d), _F32), jax.ShapeDtypeStruct((t, n_next), _BF16)] + job_shapes,
        compiler_params=pltpu.CompilerParams(
            dimension_semantics=("arbitrary",), vmem_limit_bytes=VMEM_LIMIT_BYTES),
        name="out_mlp",
    )(mixed, x2d, w_out, g_post_mix, g_pre_mlp, w_up, w_down, g_post_mlp, g_pre_mix, w_next,
      *(w_f32 for w_f32, _ in jobs))
    return out[0], out[1], out[2:]


def _attn_mlp_kernel(sink_ref, q_ref, k_ref, v_ref, x_ref, w_out_ref, g_post_mix_ref, g_pre_mlp_ref,
                     w_up_ref, w_down_ref, g_post_mlp_ref, o_ref, bias_ref, mixed_ref):
    t = pl.program_id(0)
    tiles_per_batch = k_ref.shape[1] // ATT_TILE
    n_tiles = pl.num_programs(0) - 1
    tile = jnp.minimum(t, n_tiles - 1)
    slot = t % 2

    @pl.when(t == 0)
    def _init():
        _fill_alibi_bias(bias_ref)
        mixed_ref[1] = jnp.zeros_like(mixed_ref[1])

    def write_block(j, value):
        mixed_ref[slot, j * WINDOW:(j + 1) * WINDOW, :] = value.astype(mixed_ref.dtype)

    items = _attention_items(sink_ref, q_ref, k_ref, v_ref, bias_ref,
                             (tile % tiles_per_batch) * ATT_BLOCKS_PER_STEP, write_block)

    d_ff = w_up_ref.shape[1]
    d = w_down_ref.shape[1]
    n_sub = ATT_TILE // MLP_SUBTILE
    sub_rows = [slice(j * MLP_SUBTILE, (j + 1) * MLP_SUBTILE) for j in range(n_sub)]
    vals = [{} for _ in range(n_sub)]

    def out_proj(j):
        def run():
            vals[j]["y"] = jnp.dot(mixed_ref[1 - slot, sub_rows[j], :], w_out_ref[...], preferred_element_type=_F32)
        return run

    def up(j, c):
        def run():
            v = vals[j]
            if c == 0:
                v["x"] = x_ref[sub_rows[j], :] + _rms(v["y"], g_post_mix_ref[...])
                v["h"] = _rms(v["x"], g_pre_mlp_ref[...]).astype(_BF16)
                v["a"] = []
            cols = slice(c * MLP_UP_CHUNK, (c + 1) * MLP_UP_CHUNK)
            a = jnp.maximum(jnp.dot(v["h"], w_up_ref[:, cols], preferred_element_type=_F32), 0.0)
            v["a"].append((a * a).astype(_BF16))
        return run

    def down(j, c):
        def run():
            v = vals[j]
            if c == 0:
                v["act"] = jnp.concatenate(v["a"], axis=1)
                v["z"] = []
            cols = slice(c * MLP_DOWN_CHUNK, (c + 1) * MLP_DOWN_CHUNK)
            v["z"].append(jnp.dot(v["act"], w_down_ref[:, cols], preferred_element_type=_F32))
        return run

    n_up, n_down = d_ff // MLP_UP_CHUNK, d // MLP_DOWN_CHUNK
    pieces = ([(out_proj(j), w_out_ref.shape[0] * d) for j in range(n_sub)]
              + [(up(j, c), d * MLP_UP_CHUNK) for j in range(n_sub) for c in range(n_up)]
              + [(down(j, c), d_ff * MLP_DOWN_CHUNK) for j in range(n_sub) for c in range(n_down)])
    total, done, acc = sum(weight for _, weight in pieces), 0, 0
    for piece, weight in pieces:
        piece()
        acc += weight
        target = min(len(items), -(-acc * len(items) // total))
        for item in items[done:target]:
            item()
        done = target
    for j in range(n_sub):
        z = jnp.concatenate(vals[j]["z"], axis=1)
        o_ref[sub_rows[j], :] = vals[j]["x"] + _rms(z, g_post_mlp_ref[...])


def _attn_mlp(proj, sink, x2d, layer, w_out, g_post_mix, g_pre_mlp, w_up, w_down, g_post_mlp, jobs=()):
    b, s, _ = proj.shape
    t, d = x2d.shape
    nq = ATT_Q_HEADS * ATT_HEAD_DIM
    nk = ATT_KV_HEADS * ATT_HEAD_DIM
    tiles_per_batch = s // ATT_TILE
    n_tiles = b * tiles_per_batch

    def att_tile(i):
        return jnp.minimum(i, n_tiles - 1)

    def mlp_tile(i):
        return jnp.maximum(i - 1, 0)

    job_in, job_out, job_shapes = _convert_specs(jobs, n_tiles, att_tile)
    out = pl.pallas_call(
        _with_converts(_attn_mlp_kernel, 11, 1, len(jobs)),
        grid=(n_tiles + 1,),
        in_specs=[
            pl.BlockSpec(memory_space=pltpu.SMEM),
            pl.BlockSpec((1, ATT_TILE, nq),
                         lambda i: (att_tile(i) // tiles_per_batch, att_tile(i) % tiles_per_batch, 0)),
            pl.BlockSpec((1, s, nk), lambda i: (att_tile(i) // tiles_per_batch, 0, nq // nk)),
            pl.BlockSpec((1, s, nk), lambda i: (att_tile(i) // tiles_per_batch, 0, nq // nk + 1)),
            pl.BlockSpec((ATT_TILE, d), lambda i: (mlp_tile(i), 0)),
            _resident(w_out, 0),
            _resident(g_post_mix, layer),
            _resident(g_pre_mlp, layer),
            _resident(w_up, 0),
            _resident(w_down, 0),
            _resident(g_post_mlp, layer),
        ] + job_in,
        out_specs=[pl.BlockSpec((ATT_TILE, d), lambda i: (mlp_tile(i), 0))] + job_out,
        out_shape=[jax.ShapeDtypeStruct((t, d), _F32)] + job_shapes,
        scratch_shapes=[
            pltpu.VMEM((ATT_Q_HEADS // ATT_PAIR, 3 * WINDOW, ATT_PAIR * WINDOW), _F32),
            pltpu.VMEM((2, ATT_TILE, nq), _BF16),
        ],
        compiler_params=pltpu.CompilerParams(
            dimension_semantics=("arbitrary",), vmem_limit_bytes=VMEM_LIMIT_BYTES),
        name="attn_mlp",
    )(sink, proj, proj, proj, x2d, w_out, g_post_mix, g_pre_mlp, w_up, w_down, g_post_mlp,
      *(w_f32 for w_f32, _ in jobs))
    return out[0], out[1:]


def kernel(x, w_in_ret, w_out_ret, log_decay_fwd, log_decay_bwd, w_in_attn, w_out_attn, sink_logits,
           g_pre_mix, g_post_mix, g_pre_mlp, g_post_mlp, w_up, w_down):
    b, s, d = x.shape
    x2d = x.reshape(b * s, d)
    g_pre_mix, g_post_mix, g_pre_mlp, g_post_mlp = (
        g.reshape(DEPTH, 1, d) for g in (g_pre_mix, g_post_mix, g_pre_mlp, g_post_mlp))
    w_in = w_in_ret[0:1].astype(_BF16)
    for layer in range(0, DEPTH, 2):
        i = layer // 2
        proj, (w_out, w_up_a, w_down_a) = _norm_proj(
            x2d, g_pre_mix, layer, w_in, 0, jobs=[(w_out_ret, i), (w_up, layer), (w_down, layer)])
        mixed, (w_in_b, w_out_b) = _retention_core(
            proj.reshape(b, s, -1), log_decay_fwd[i], log_decay_bwd[i], jobs=[(w_in_attn, i), (w_out_attn, i)])
        x2d, proj, (w_up_b, w_down_b) = _out_mlp(
            mixed.reshape(b * s, -1), x2d, layer, w_out, g_post_mix, g_pre_mlp, w_up_a, w_down_a, g_post_mlp,
            g_pre_mix, w_in_b, jobs=[(w_up, layer + 1), (w_down, layer + 1)])
        more = layer + 2 < DEPTH
        x2d, next_w_in = _attn_mlp(
            proj.reshape(b, s, -1), sink_logits[i], x2d, layer + 1, w_out_b, g_post_mix, g_pre_mlp, w_up_b,
            w_down_b, g_post_mlp, jobs=[(w_in_ret, i + 1)] if more else [])
        if more:
            (w_in,) = next_w_in
    return x2d.reshape(b, s, d)
```

```python
import jax
import jax.numpy as jnp
from jax import lax
from jax.experimental import pallas as pl
from jax.experimental.pallas import tpu as pltpu

D_MODEL = 1024
DEPTH = 4
RET_HEADS = 4
RET_QK_DIM = 256
RET_V_DIM = 512
ATT_Q_HEADS = 16
ATT_KV_HEADS = 4
ATT_GROUP = ATT_Q_HEADS // ATT_KV_HEADS
ATT_HEAD_DIM = 64
WINDOW = 128
D_FF = 4 * D_MODEL
EPS = 1e-6
NEG_INF = -1e30
LOG2E = 1.4426950408889634

TOKEN_TILE = 1024
PROJ_SUBTILE = 256
MLP_TILE = 512
MLP_SUBTILE = 256
RET_CHUNK = 256
RET_SCORES_AHEAD = 2
ATT_PAIR = 2
ATT_ONES_PAD = 16
ATT_BLOCKS_PER_STEP = 4
ATT_TILE = ATT_BLOCKS_PER_STEP * WINDOW
ATT_SCORES_AHEAD = 2
MLP_UP_CHUNK = 512
MLP_DOWN_CHUNK = 256
VMEM_LIMIT_BYTES = 56 * 1024 * 1024

_BF16 = jnp.bfloat16
_F32 = jnp.float32


def _rms(x, g):
    y = x * lax.rsqrt(jnp.mean(x * x, axis=-1, keepdims=True) + EPS)
    return y * g


def _resident(stacked, index):
    zeros = (0,) * (stacked.ndim - 1)
    return pl.BlockSpec((None,) + stacked.shape[1:], lambda *_: (index,) + zeros, pipeline_mode=pl.Buffered(1))


def _convert_specs(jobs, n_steps, step_of):
    in_specs, out_specs, out_shapes = [], [], []
    for w, index in jobs:
        _, k, n = w.shape
        rows = k // n_steps
        assert rows * n_steps == k and rows % 16 == 0, (w.shape, n_steps)
        in_specs.append(pl.BlockSpec((None, rows, n), lambda *g, index=index: (index, step_of(*g), 0)))
        out_specs.append(pl.BlockSpec((None, rows, n), lambda *g: (0, step_of(*g), 0)))
        out_shapes.append(jax.ShapeDtypeStruct((1, k, n), _BF16))
    return in_specs, out_specs, out_shapes


def _with_converts(body, n_in, n_out, n_jobs):
    def kernel(*refs):
        ins, rest = refs[:n_in], refs[n_in:]
        job_ins, rest = rest[:n_jobs], rest[n_jobs:]
        outs, rest = rest[:n_out], rest[n_out:]
        job_outs, scratch = rest[:n_jobs], rest[n_jobs:]
        body(*ins, *outs, *scratch)
        for src, dst in zip(job_ins, job_outs):
            dst[...] = src[...].astype(dst.dtype)
    return kernel


def _norm_proj_kernel(x_ref, g_ref, w_ref, o_ref):
    for j in range(x_ref.shape[0] // PROJ_SUBTILE):
        r = slice(j * PROJ_SUBTILE, (j + 1) * PROJ_SUBTILE)
        h = _rms(x_ref[r, :], g_ref[...]).astype(_BF16)
        o_ref[r, :] = jnp.dot(h, w_ref[...], preferred_element_type=_F32).astype(o_ref.dtype)


def _norm_proj(x2d, g, layer, w, w_index, jobs=()):
    t, d = x2d.shape
    n = w.shape[2]
    n_steps = t // TOKEN_TILE
    job_in, job_out, job_shapes = _convert_specs(jobs, n_steps, lambda i: i)
    out = pl.pallas_call(
        _with_converts(_norm_proj_kernel, 3, 1, len(jobs)),
        grid=(n_steps,),
        in_specs=[
            pl.BlockSpec((TOKEN_TILE, d), lambda i: (i, 0)),
            _resident(g, layer),
            _resident(w, w_index),
        ] + job_in,
        out_specs=[pl.BlockSpec((TOKEN_TILE, n), lambda i: (i, 0))] + job_out,
        out_shape=[jax.ShapeDtypeStruct((t, n), _BF16)] + job_shapes,
        compiler_params=pltpu.CompilerParams(
            dimension_semantics=("arbitrary",), vmem_limit_bytes=VMEM_LIMIT_BYTES),
        name="norm_proj",
    )(x2d, g, w, *(w_f32 for w_f32, _ in jobs))
    return out[0], out[1:]


def _retention_kernel(lgf_ref, lgb_ref, q_ref, k_ref, v_ref, g_ref, o_ref, rhs_ref, state_ref):
    c = RET_CHUNK
    dk = RET_QK_DIM
    seq = q_ref.shape[1]
    n_chunks = seq // c
    head = pl.program_id(1)
    lf = lgf_ref[head]
    lb = lgb_ref[head]
    k_scale = dk ** -0.5

    row = lax.broadcasted_iota(jnp.int32, (c, c), 0)
    col = lax.broadcasted_iota(jnp.int32, (c, c), 1)
    diff = (row - col).astype(_F32)
    inner_decay = jnp.where(
        diff >= 0.0,
        jnp.exp(lf * jnp.maximum(diff, 0.0)),
        jnp.exp(lb * jnp.maximum(-diff, 0.0))) * k_scale
    idx = lax.broadcasted_iota(jnp.int32, (c, dk), 0).astype(_F32)
    q_decay_f = jnp.exp(lf * (idx + 1.0)).astype(_BF16)
    k_decay_f = (jnp.exp(lf * (c - 1.0 - idx)) * k_scale).astype(_BF16)
    q_decay_b = jnp.exp(lb * (c - idx)).astype(_BF16)
    k_decay_b = (jnp.exp(lb * idx) * k_scale).astype(_BF16)
    chunk_decay_f = jnp.exp(lf * c)
    chunk_decay_b = jnp.exp(lb * c)

    def rows(i):
        return slice(i * c, (i + 1) * c)

    def kv_product(i, k_decay):
        return lax.dot_general(k_ref[0, rows(i), :] * k_decay, v_ref[0, rows(i), :], (((0,), (0,)), ((), ())),
                               preferred_element_type=_F32)

    state_ref[...] = jnp.zeros_like(state_ref)
    for i in range(n_chunks):
        rhs_ref[i, 0:c, :] = v_ref[0, rows(i), :]
        rhs_ref[i, c:c + dk, :] = state_ref[...].astype(_BF16)
        if i + 1 < n_chunks:
            state_ref[...] = state_ref[...] * chunk_decay_f + kv_product(i, k_decay_f)
    state_ref[...] = jnp.zeros_like(state_ref)
    for i in reversed(range(n_chunks)):
        rhs_ref[i, c + dk:c + 2 * dk, :] = state_ref[...].astype(_BF16)
        if i > 0:
            state_ref[...] = state_ref[...] * chunk_decay_b + kv_product(i, k_decay_b)

    def chunk_scores(i):
        return lax.dot_general(q_ref[0, rows(i), :], k_ref[0, rows(i), :], (((1,), (1,)), ((), ())),
                               preferred_element_type=_F32)

    pending = [chunk_scores(i) for i in range(min(RET_SCORES_AHEAD, n_chunks))]
    for i in range(n_chunks):
        if i + RET_SCORES_AHEAD < n_chunks:
            pending.append(chunk_scores(i + RET_SCORES_AHEAD))
        s = pending.pop(0)
        q = q_ref[0, rows(i), :]
        lhs = jnp.concatenate([(s * inner_decay).astype(_BF16), q * q_decay_f, q * q_decay_b], axis=1)
        o = jnp.dot(lhs, rhs_ref[i], preferred_element_type=_F32)
        o = o * lax.rsqrt(jnp.mean(o * o, axis=-1, keepdims=True) + EPS)
        gate = g_ref[0, rows(i), :].astype(_F32)
        silu = gate / (1.0 + jnp.exp2(gate * -LOG2E))
        o_ref[0, rows(i), :] = (silu * o).astype(o_ref.dtype)


def _retention_core(proj, log_decay_fwd, log_decay_bwd, jobs=()):
    b, s, _ = proj.shape
    dk, dv, h = RET_QK_DIM, RET_V_DIM, RET_HEADS
    k_off = h * dk // dk
    v_off = 2 * h * dk // dv
    g_off = (2 * h * dk + h * dv) // dv
    smem = pl.BlockSpec(memory_space=pltpu.SMEM)
    job_in, job_out, job_shapes = _convert_specs(jobs, b * h, lambda bi, hi: bi * h + hi)
    out = pl.pallas_call(
        _with_converts(_retention_kernel, 6, 1, len(jobs)),
        grid=(b, h),
        in_specs=[
            smem, smem,
            pl.BlockSpec((1, s, dk), lambda bi, hi: (bi, 0, hi)),
            pl.BlockSpec((1, s, dk), lambda bi, hi: (bi, 0, k_off + hi)),
            pl.BlockSpec((1, s, dv), lambda bi, hi: (bi, 0, v_off + hi)),
            pl.BlockSpec((1, s, dv), lambda bi, hi: (bi, 0, g_off + hi)),
        ] + job_in,
        out_specs=[pl.BlockSpec((1, s, dv), lambda bi, hi: (bi, 0, hi))] + job_out,
        out_shape=[jax.ShapeDtypeStruct((b, s, h * dv), _BF16)] + job_shapes,
        scratch_shapes=[
            pltpu.VMEM((s // RET_CHUNK, RET_CHUNK + 2 * dk, dv), _BF16),
            pltpu.VMEM((dk, dv), _F32),
        ],
        compiler_params=pltpu.CompilerParams(
            dimension_semantics=("arbitrary", "arbitrary"), vmem_limit_bytes=VMEM_LIMIT_BYTES),
        name="retention_core",
    )(log_decay_fwd, log_decay_bwd, proj, proj, proj, proj, *(w_f32 for w_f32, _ in jobs))
    return out[0], out[1:]


def _fill_alibi_bias(bias_ref):
    w, pair = WINDOW, ATT_PAIR
    second = lax.broadcasted_iota(jnp.int32, (1, pair * w), 1) >= w
    key = lax.broadcasted_iota(jnp.int32, (3 * w, pair * w), 0)
    qry = lax.broadcasted_iota(jnp.int32, (3 * w, pair * w), 1) & (w - 1)
    dist = jnp.abs(w + qry - key).astype(_F32)
    for i in range(ATT_Q_HEADS // pair):
        slope = [2.0 ** (-8.0 * (pair * i + j + 1) / ATT_Q_HEADS) for j in range(pair)]
        bias_ref[i] = (-jnp.where(second, slope[1], slope[0]) * dist) * LOG2E


def _attention_items(sink_ref, q_ref, k_ref, v_ref, bias_ref, first_block, write_block):
    w = WINDOW
    d = ATT_HEAD_DIM
    grp = ATT_GROUP
    pair = ATT_PAIR
    n_pairs = ATT_Q_HEADS // pair
    nb = k_ref.shape[1] // w

    lane_qry = lax.broadcasted_iota(jnp.int32, (1, pair * w), 1)
    second = lane_qry >= w

    def rows(i):
        return pl.ds(pl.multiple_of(i * w, w), w)

    key_minus_qry = lax.broadcasted_iota(jnp.int32, (w, pair * w), 0) - (lane_qry & (w - 1))
    never = 2 * w
    ones_pad = jnp.ones((w, ATT_ONES_PAD), _BF16)

    class Block:
        def __init__(self, j):
            n = first_block + j
            self.j = j
            self.q_rows = slice(j * w, (j + 1) * w)
            self.key_rows = [rows(jnp.maximum(n - 1, 0)), rows(n), rows(jnp.minimum(n + 1, nb - 1))]
            self.prev_bound = jnp.where(n > 0, 0, never)
            self.next_bound = jnp.where(n < nb - 1, 0, never)
            self.v_ones = [[jnp.concatenate([v_ref[0, r, h * d:(h + 1) * d], ones_pad], axis=1)
                            for r in self.key_rows] for h in range(ATT_KV_HEADS)]

    def scores(blk, i):
        h = i * pair // grp
        ks = slice(h * d, (h + 1) * d)
        k_band = jnp.concatenate([k_ref[0, r, ks] for r in blk.key_rows], axis=0)
        q_pair = jnp.concatenate([q_ref[0, blk.q_rows, hd * d:(hd + 1) * d]
                                  for hd in range(pair * i, pair * (i + 1))], axis=0)
        q_pair = (q_pair.astype(_F32) * (d ** -0.5 * LOG2E)).astype(_BF16)
        return lax.dot_general(k_band, q_pair, (((1,), (1,)), ((), ())), preferred_element_type=_F32)

    def attend(blk, i, s):
        h = i * pair // grp
        sink = jnp.where(second, sink_ref[pair * i + 1], sink_ref[pair * i]) * LOG2E
        maxes, accs = [], []
        for c in range(3):
            t = s[c * w:(c + 1) * w] + bias_ref[i, c * w:(c + 1) * w, :]
            if c == 0:
                t = jnp.where(key_minus_qry >= blk.prev_bound, t, NEG_INF * LOG2E)
            if c == 2:
                t = jnp.where(-key_minus_qry >= blk.next_bound, t, NEG_INF * LOG2E)
            m = jnp.maximum(jnp.max(t, axis=0, keepdims=True), sink)
            p = jnp.exp2(t - m)
            maxes.append(m)
            accs.append(lax.dot_general(blk.v_ones[h][c], p.astype(_BF16), (((0,), (0,)), ((), ())),
                                        preferred_element_type=_F32))
        m_all = jnp.maximum(jnp.maximum(maxes[0], maxes[1]), maxes[2])
        weights = [jnp.exp2(m - m_all) for m in maxes]
        total = accs[0] * weights[0] + accs[1] * weights[1] + accs[2] * weights[2]
        return total[0:d] / (total[d:d + 1] + jnp.exp2(sink - m_all))

    blocks = [Block(j) for j in range(ATT_BLOCKS_PER_STEP)]
    work = [(blk, i) for blk in blocks for i in range(n_pairs)]
    pending = [scores(*item) for item in work[:ATT_SCORES_AHEAD]]
    pieces = []

    def make_item(k):
        def run():
            blk, i = work[k]
            if k + ATT_SCORES_AHEAD < len(work):
                pending.append(scores(*work[k + ATT_SCORES_AHEAD]))
            out = attend(blk, i, pending.pop(0))
            pieces.extend(out[:, j * w:(j + 1) * w] for j in range(pair))
            if i == n_pairs - 1:
                write_block(blk.j, jnp.concatenate(pieces, axis=0).T)
                pieces.clear()
        return run

    return [make_item(k) for k in range(len(work))]


def _out_mlp_kernel(m_ref, x_ref, w_out_ref, g_post_mix_ref, g_pre_mlp_ref, w_up_ref, w_down_ref,
                    g_post_mlp_ref, g_next_ref, w_next_ref, o_ref, p_ref):
    n_sub = m_ref.shape[0] // MLP_SUBTILE
    rows = [slice(j * MLP_SUBTILE, (j + 1) * MLP_SUBTILE) for j in range(n_sub)]
    ys = [jnp.dot(m_ref[r, :], w_out_ref[...], preferred_element_type=_F32) for r in rows]
    xs, acts = [], []
    for r, y in zip(rows, ys):
        x = x_ref[r, :] + _rms(y, g_post_mix_ref[...])
        h = _rms(x, g_pre_mlp_ref[...]).astype(_BF16)
        xs.append(x)
        acts.append(jnp.dot(h, w_up_ref[...], preferred_element_type=_F32))
    zs = []
    for a in acts:
        a = jnp.maximum(a, 0.0)
        zs.append(jnp.dot((a * a).astype(_BF16), w_down_ref[...], preferred_element_type=_F32))
    outs = []
    for r, x, z in zip(rows, xs, zs):
        outs.append(x + _rms(z, g_post_mlp_ref[...]))
        o_ref[r, :] = outs[-1]
    for r, out in zip(rows, outs):
        h = _rms(out, g_next_ref[...]).astype(_BF16)
        p_ref[r, :] = jnp.dot(h, w_next_ref[...], preferred_element_type=_F32).astype(p_ref.dtype)


def _out_mlp(mixed, x2d, layer, w_out, g_post_mix, g_pre_mlp, w_up, w_down, g_post_mlp, g_pre_mix, w_next, jobs=()):
    t, d = x2d.shape
    dm = mixed.shape[1]
    n_next = w_next.shape[2]
    n_steps = t // MLP_TILE
    job_in, job_out, job_shapes = _convert_specs(jobs, n_steps, lambda i: i)
    out = pl.pallas_call(
        _with_converts(_out_mlp_kernel, 10, 2, len(jobs)),
        grid=(n_steps,),
        in_specs=[
            pl.BlockSpec((MLP_TILE, dm), lambda i: (i, 0)),
            pl.BlockSpec((MLP_TILE, d), lambda i: (i, 0)),
            _resident(w_out, 0),
            _resident(g_post_mix, layer),
            _resident(g_pre_mlp, layer),
            _resident(w_up, 0),
            _resident(w_down, 0),
            _resident(g_post_mlp, layer),
            _resident(g_pre_mix, layer + 1),
            _resident(w_next, 0),
        ] + job_in,
        out_specs=[pl.BlockSpec((MLP_TILE, d), lambda i: (i, 0)),
                   pl.BlockSpec((MLP_TILE, n_next), lambda i: (i, 0))] + job_out,
        out_shape=[jax.ShapeDtypeStruct((t, d), _F32), jax.ShapeDtypeStruct((t, n_next), _BF16)] + job_shapes,
        compiler_params=pltpu.CompilerParams(
            dimension_semantics=("arbitrary",), vmem_limit_bytes=VMEM_LIMIT_BYTES),
        name="out_mlp",
    )(mixed, x2d, w_out, g_post_mix, g_pre_mlp, w_up, w_down, g_post_mlp, g_pre_mix, w_next,
      *(w_f32 for w_f32, _ in jobs))
    return out[0], out[1], out[2:]


def _attn_mlp_kernel(sink_ref, q_ref, k_ref, v_ref, x_ref, w_out_ref, g_post_mix_ref, g_pre_mlp_ref,
                     w_up_ref, w_down_ref, g_post_mlp_ref, o_ref, bias_ref, mixed_ref):
    t = pl.program_id(0)
    tiles_per_batch = k_ref.shape[1] // ATT_TILE
    n_tiles = pl.num_programs(0) - 1
    slot = t % 2

    def write_block(j, value):
        mixed_ref[slot, j * WINDOW:(j + 1) * WINDOW, :] = value.astype(mixed_ref.dtype)

    def step(with_attention, with_mlp):
        items = []
        if with_attention:
            items = _attention_items(sink_ref, q_ref, k_ref, v_ref, bias_ref,
                                     (t % tiles_per_batch) * ATT_BLOCKS_PER_STEP, write_block)
        if not with_mlp:
            for item in items:
                item()
            return

        d_ff = w_up_ref.shape[1]
        d = w_down_ref.shape[1]
        n_sub = ATT_TILE // MLP_SUBTILE
        sub_rows = [slice(j * MLP_SUBTILE, (j + 1) * MLP_SUBTILE) for j in range(n_sub)]
        vals = [{} for _ in range(n_sub)]

        def out_proj(j):
            def run():
                vals[j]["y"] = jnp.dot(mixed_ref[1 - slot, sub_rows[j], :], w_out_ref[...],
                                       preferred_element_type=_F32)
            return run

        def up(j, c):
            def run():
                v = vals[j]
                if c == 0:
                    v["x"] = x_ref[sub_rows[j], :] + _rms(v["y"], g_post_mix_ref[...])
                    v["h"] = _rms(v["x"], g_pre_mlp_ref[...]).astype(_BF16)
                    v["a"] = []
                cols = slice(c * MLP_UP_CHUNK, (c + 1) * MLP_UP_CHUNK)
                a = jnp.maximum(jnp.dot(v["h"], w_up_ref[:, cols], preferred_element_type=_F32), 0.0)
                v["a"].append((a * a).astype(_BF16))
            return run

        def down(j, c):
            def run():
                v = vals[j]
                if c == 0:
                    v["act"] = jnp.concatenate(v["a"], axis=1)
                    v["z"] = []
                cols = slice(c * MLP_DOWN_CHUNK, (c + 1) * MLP_DOWN_CHUNK)
                v["z"].append(jnp.dot(v["act"], w_down_ref[:, cols], preferred_element_type=_F32))
            return run

        n_up, n_down = d_ff // MLP_UP_CHUNK, d // MLP_DOWN_CHUNK
        pieces = ([(out_proj(j), w_out_ref.shape[0] * d) for j in range(n_sub)]
                  + [(up(j, c), d * MLP_UP_CHUNK) for j in range(n_sub) for c in range(n_up)]
                  + [(down(j, c), d_ff * MLP_DOWN_CHUNK) for j in range(n_sub) for c in range(n_down)])
        total, done, acc = sum(weight for _, weight in pieces), 0, 0
        for piece, weight in pieces:
            piece()
            acc += weight
            target = min(len(items), -(-acc * len(items) // total))
            for item in items[done:target]:
                item()
            done = target
        for j in range(n_sub):
            z = jnp.concatenate(vals[j]["z"], axis=1)
            o_ref[sub_rows[j], :] = vals[j]["x"] + _rms(z, g_post_mlp_ref[...])

    @pl.when(t == 0)
    def _first():
        _fill_alibi_bias(bias_ref)
        step(True, False)

    @pl.when((t > 0) & (t < n_tiles))
    def _middle():
        step(True, True)

    @pl.when(t == n_tiles)
    def _last():
        step(False, True)


def _attn_mlp(proj, sink, x2d, layer, w_out, g_post_mix, g_pre_mlp, w_up, w_down, g_post_mlp, jobs=()):
    b, s, _ = proj.shape
    t, d = x2d.shape
    nq = ATT_Q_HEADS * ATT_HEAD_DIM
    nk = ATT_KV_HEADS * ATT_HEAD_DIM
    tiles_per_batch = s // ATT_TILE
    n_tiles = b * tiles_per_batch

    def att_tile(i):
        return jnp.minimum(i, n_tiles - 1)

    def mlp_tile(i):
        return jnp.maximum(i - 1, 0)

    job_in, job_out, job_shapes = _convert_specs(jobs, n_tiles, att_tile)
    out = pl.pallas_call(
        _with_converts(_attn_mlp_kernel, 11, 1, len(jobs)),
        grid=(n_tiles + 1,),
        in_specs=[
            pl.BlockSpec(memory_space=pltpu.SMEM),
            pl.BlockSpec((1, ATT_TILE, nq),
                         lambda i: (att_tile(i) // tiles_per_batch, att_tile(i) % tiles_per_batch, 0)),
            pl.BlockSpec((1, s, nk), lambda i: (att_tile(i) // tiles_per_batch, 0, nq // nk)),
            pl.BlockSpec((1, s, nk), lambda i: (att_tile(i) // tiles_per_batch, 0, nq // nk + 1)),
            pl.BlockSpec((ATT_TILE, d), lambda i: (mlp_tile(i), 0)),
            _resident(w_out, 0),
            _resident(g_post_mix, layer),
            _resident(g_pre_mlp, layer),
            _resident(w_up, 0),
            _resident(w_down, 0),
            _resident(g_post_mlp, layer),
        ] + job_in,
        out_specs=[pl.BlockSpec((ATT_TILE, d), lambda i: (mlp_tile(i), 0))] + job_out,
        out_shape=[jax.ShapeDtypeStruct((t, d), _F32)] + job_shapes,
        scratch_shapes=[
            pltpu.VMEM((ATT_Q_HEADS // ATT_PAIR, 3 * WINDOW, ATT_PAIR * WINDOW), _F32),
            pltpu.VMEM((2, ATT_TILE, nq), _BF16),
        ],
        compiler_params=pltpu.CompilerParams(
            dimension_semantics=("arbitrary",), vmem_limit_bytes=VMEM_LIMIT_BYTES),
        name="attn_mlp",
    )(sink, proj, proj, proj, x2d, w_out, g_post_mix, g_pre_mlp, w_up, w_down, g_post_mlp,
      *(w_f32 for w_f32, _ in jobs))
    return out[0], out[1:]


def kernel(x, w_in_ret, w_out_ret, log_decay_fwd, log_decay_bwd, w_in_attn, w_out_attn, sink_logits,
           g_pre_mix, g_post_mix, g_pre_mlp, g_post_mlp, w_up, w_down):
    b, s, d = x.shape
    x2d = x.reshape(b * s, d)
    g_pre_mix, g_post_mix, g_pre_mlp, g_post_mlp = (
        g.reshape(DEPTH, 1, d) for g in (g_pre_mix, g_post_mix, g_pre_mlp, g_post_mlp))
    w_in = w_in_ret[0:1].astype(_BF16)
    for layer in range(0, DEPTH, 2):
        i = layer // 2
        proj, (w_out, w_up_a, w_down_a) = _norm_proj(
            x2d, g_pre_mix, layer, w_in, 0, jobs=[(w_out_ret, i), (w_up, layer), (w_down, layer)])
        mixed, (w_in_b, w_out_b) = _retention_core(
            proj.reshape(b, s, -1), log_decay_fwd[i], log_decay_bwd[i], jobs=[(w_in_attn, i), (w_out_attn, i)])
        x2d, proj, (w_up_b, w_down_b) = _out_mlp(
            mixed.reshape(b * s, -1), x2d, layer, w_out, g_post_mix, g_pre_mlp, w_up_a, w_down_a, g_post_mlp,
            g_pre_mix, w_in_b, jobs=[(w_up, layer + 1), (w_down, layer + 1)])
        more = layer + 2 < DEPTH
        x2d, next_w_in = _attn_mlp(
            proj.reshape(b, s, -1), sink_logits[i], x2d, layer + 1, w_out_b, g_post_mix, g_pre_mlp, w_up_b,
            w_down_b, g_post_mlp, jobs=[(w_in_ret, i + 1)] if more else [])
        if more:
            (w_in,) = next_w_in
    return x2d.reshape(b, s, d)
```

```python
import jax
import jax.numpy as jnp
from jax import lax
from jax.experimental import pallas as pl
from jax.experimental.pallas import tpu as pltpu

D_MODEL = 1024
DEPTH = 4
RET_HEADS = 4
RET_QK_DIM = 256
RET_V_DIM = 512
ATT_Q_HEADS = 16
ATT_KV_HEADS = 4
ATT_GROUP = ATT_Q_HEADS // ATT_KV_HEADS
ATT_HEAD_DIM = 64
WINDOW = 128
D_FF = 4 * D_MODEL
EPS = 1e-6
NEG_INF = -1e30
LOG2E = 1.4426950408889634

TOKEN_TILE = 1024
PROJ_SUBTILE = 256
MLP_TILE = 512
MLP_SUBTILE = 256
RET_CHUNK = 256
RET_SCORES_AHEAD = 4
ATT_PAIR = 2
ATT_ONES_PAD = 16
ATT_BLOCKS_PER_STEP = 4
ATT_TILE = ATT_BLOCKS_PER_STEP * WINDOW
ATT_SCORES_AHEAD = 2
ATT_TAIL_ITEMS = 2
MLP_UP_CHUNK = 512
MLP_DOWN_CHUNK = 256
VMEM_LIMIT_BYTES = 56 * 1024 * 1024

_BF16 = jnp.bfloat16
_F32 = jnp.float32


def _rms(x, g):
    y = x * lax.rsqrt(jnp.mean(x * x, axis=-1, keepdims=True) + EPS)
    return y * g


def _resident(stacked, index):
    zeros = (0,) * (stacked.ndim - 1)
    return pl.BlockSpec((None,) + stacked.shape[1:], lambda *_: (index,) + zeros, pipeline_mode=pl.Buffered(1))


def _convert_specs(jobs, n_steps, step_of):
    in_specs, out_specs, out_shapes = [], [], []
    for w, index in jobs:
        _, k, n = w.shape
        rows = k // n_steps
        assert rows * n_steps == k and rows % 16 == 0, (w.shape, n_steps)
        in_specs.append(pl.BlockSpec((None, rows, n), lambda *g, index=index: (index, step_of(*g), 0)))
        out_specs.append(pl.BlockSpec((None, rows, n), lambda *g: (0, step_of(*g), 0)))
        out_shapes.append(jax.ShapeDtypeStruct((1, k, n), _BF16))
    return in_specs, out_specs, out_shapes


def _with_converts(body, n_in, n_out, n_jobs):
    def kernel(*refs):
        ins, rest = refs[:n_in], refs[n_in:]
        job_ins, rest = rest[:n_jobs], rest[n_jobs:]
        outs, rest = rest[:n_out], rest[n_out:]
        job_outs, scratch = rest[:n_jobs], rest[n_jobs:]
        body(*ins, *outs, *scratch)
        for src, dst in zip(job_ins, job_outs):
            dst[...] = src[...].astype(dst.dtype)
    return kernel


def _norm_proj_kernel(x_ref, g_ref, w_ref, o_ref):
    for j in range(x_ref.shape[0] // PROJ_SUBTILE):
        r = slice(j * PROJ_SUBTILE, (j + 1) * PROJ_SUBTILE)
        h = _rms(x_ref[r, :], g_ref[...]).astype(_BF16)
        o_ref[r, :] = jnp.dot(h, w_ref[...], preferred_element_type=_F32).astype(o_ref.dtype)


def _norm_proj(x2d, g, layer, w, w_index, jobs=()):
    t, d = x2d.shape
    n = w.shape[2]
    n_steps = t // TOKEN_TILE
    job_in, job_out, job_shapes = _convert_specs(jobs, n_steps, lambda i: i)
    out = pl.pallas_call(
        _with_converts(_norm_proj_kernel, 3, 1, len(jobs)),
        grid=(n_steps,),
        in_specs=[
            pl.BlockSpec((TOKEN_TILE, d), lambda i: (i, 0)),
            _resident(g, layer),
            _resident(w, w_index),
        ] + job_in,
        out_specs=[pl.BlockSpec((TOKEN_TILE, n), lambda i: (i, 0))] + job_out,
        out_shape=[jax.ShapeDtypeStruct((t, n), _BF16)] + job_shapes,
        compiler_params=pltpu.CompilerParams(
            dimension_semantics=("arbitrary",), vmem_limit_bytes=VMEM_LIMIT_BYTES),
        name="norm_proj",
    )(x2d, g, w, *(w_f32 for w_f32, _ in jobs))
    return out[0], out[1:]


def _retention_kernel(lgf_ref, lgb_ref, q_ref, k_ref, v_ref, g_ref, o_ref, rhs_ref, state_ref):
    c = RET_CHUNK
    dk = RET_QK_DIM
    seq = q_ref.shape[1]
    n_chunks = seq // c
    head = pl.program_id(1)
    lf = lgf_ref[head]
    lb = lgb_ref[head]
    k_scale = dk ** -0.5

    row = lax.broadcasted_iota(jnp.int32, (c, c), 0)
    col = lax.broadcasted_iota(jnp.int32, (c, c), 1)
    diff = (row - col).astype(_F32)
    inner_decay = jnp.where(
        diff >= 0.0,
        jnp.exp(lf * jnp.maximum(diff, 0.0)),
        jnp.exp(lb * jnp.maximum(-diff, 0.0))) * k_scale
    idx = lax.broadcasted_iota(jnp.int32, (c, dk), 0).astype(_F32)
    q_decay_f = jnp.exp(lf * (idx + 1.0)).astype(_BF16)
    k_decay_f = (jnp.exp(lf * (c - 1.0 - idx)) * k_scale).astype(_BF16)
    q_decay_b = jnp.exp(lb * (c - idx)).astype(_BF16)
    k_decay_b = (jnp.exp(lb * idx) * k_scale).astype(_BF16)
    chunk_decay_f = jnp.exp(lf * c)
    chunk_decay_b = jnp.exp(lb * c)

    def rows(i):
        return slice(i * c, (i + 1) * c)

    def kv_product(i, k_decay):
        return lax.dot_general(k_ref[0, rows(i), :] * k_decay, v_ref[0, rows(i), :], (((0,), (0,)), ((), ())),
                               preferred_element_type=_F32)

    state_ref[...] = jnp.zeros_like(state_ref)
    for i in range(n_chunks):
        rhs_ref[i, 0:c, :] = v_ref[0, rows(i), :]
        rhs_ref[i, c:c + dk, :] = state_ref[...].astype(_BF16)
        if i + 1 < n_chunks:
            state_ref[...] = state_ref[...] * chunk_decay_f + kv_product(i, k_decay_f)
    state_ref[...] = jnp.zeros_like(state_ref)
    for i in reversed(range(n_chunks)):
        rhs_ref[i, c + dk:c + 2 * dk, :] = state_ref[...].astype(_BF16)
        if i > 0:
            state_ref[...] = state_ref[...] * chunk_decay_b + kv_product(i, k_decay_b)

    def chunk_scores(i):
        return lax.dot_general(q_ref[0, rows(i), :], k_ref[0, rows(i), :], (((1,), (1,)), ((), ())),
                               preferred_element_type=_F32)

    pending = [chunk_scores(i) for i in range(min(RET_SCORES_AHEAD, n_chunks))]
    for i in range(n_chunks):
        if i + RET_SCORES_AHEAD < n_chunks:
            pending.append(chunk_scores(i + RET_SCORES_AHEAD))
        s = pending.pop(0)
        q = q_ref[0, rows(i), :]
        lhs = jnp.concatenate([(s * inner_decay).astype(_BF16), q * q_decay_f, q * q_decay_b], axis=1)
        o = jnp.dot(lhs, rhs_ref[i], preferred_element_type=_F32)
        o = o * lax.rsqrt(jnp.mean(o * o, axis=-1, keepdims=True) + EPS)
        gate = g_ref[0, rows(i), :].astype(_F32)
        silu = gate / (1.0 + jnp.exp2(gate * -LOG2E))
        o_ref[0, rows(i), :] = (silu * o).astype(o_ref.dtype)


def _retention_core(proj, log_decay_fwd, log_decay_bwd, jobs=()):
    b, s, _ = proj.shape
    dk, dv, h = RET_QK_DIM, RET_V_DIM, RET_HEADS
    k_off = h * dk // dk
    v_off = 2 * h * dk // dv
    g_off = (2 * h * dk + h * dv) // dv
    smem = pl.BlockSpec(memory_space=pltpu.SMEM)
    job_in, job_out, job_shapes = _convert_specs(jobs, b * h, lambda bi, hi: bi * h + hi)
    out = pl.pallas_call(
        _with_converts(_retention_kernel, 6, 1, len(jobs)),
        grid=(b, h),
        in_specs=[
            smem, smem,
            pl.BlockSpec((1, s, dk), lambda bi, hi: (bi, 0, hi)),
            pl.BlockSpec((1, s, dk), lambda bi, hi: (bi, 0, k_off + hi)),
            pl.BlockSpec((1, s, dv), lambda bi, hi: (bi, 0, v_off + hi)),
            pl.BlockSpec((1, s, dv), lambda bi, hi: (bi, 0, g_off + hi)),
        ] + job_in,
        out_specs=[pl.BlockSpec((1, s, dv), lambda bi, hi: (bi, 0, hi))] + job_out,
        out_shape=[jax.ShapeDtypeStruct((b, s, h * dv), _BF16)] + job_shapes,
        scratch_shapes=[
            pltpu.VMEM((s // RET_CHUNK, RET_CHUNK + 2 * dk, dv), _BF16),
            pltpu.VMEM((dk, dv), _F32),
        ],
        compiler_params=pltpu.CompilerParams(
            dimension_semantics=("arbitrary", "arbitrary"), vmem_limit_bytes=VMEM_LIMIT_BYTES),
        name="retention_core",
    )(log_decay_fwd, log_decay_bwd, proj, proj, proj, proj, *(w_f32 for w_f32, _ in jobs))
    return out[0], out[1:]


def _fill_alibi_bias(bias_ref):
    w, pair = WINDOW, ATT_PAIR
    second = lax.broadcasted_iota(jnp.int32, (1, pair * w), 1) >= w
    key = lax.broadcasted_iota(jnp.int32, (3 * w, pair * w), 0)
    qry = lax.broadcasted_iota(jnp.int32, (3 * w, pair * w), 1) & (w - 1)
    dist = jnp.abs(w + qry - key).astype(_F32)
    for i in range(ATT_Q_HEADS // pair):
        slope = [2.0 ** (-8.0 * (pair * i + j + 1) / ATT_Q_HEADS) for j in range(pair)]
        bias_ref[i] = (-jnp.where(second, slope[1], slope[0]) * dist) * LOG2E


def _attention_items(sink_ref, q_ref, k_ref, v_ref, bias_ref, first_block, write_block):
    w = WINDOW
    d = ATT_HEAD_DIM
    grp = ATT_GROUP
    pair = ATT_PAIR
    n_pairs = ATT_Q_HEADS // pair
    nb = k_ref.shape[1] // w

    lane_qry = lax.broadcasted_iota(jnp.int32, (1, pair * w), 1)
    second = lane_qry >= w

    def rows(i):
        return pl.ds(pl.multiple_of(i * w, w), w)

    key_minus_qry = lax.broadcasted_iota(jnp.int32, (w, pair * w), 0) - (lane_qry & (w - 1))
    never = 2 * w
    ones_pad = jnp.ones((w, ATT_ONES_PAD), _BF16)

    class Block:
        def __init__(self, j):
            n = first_block + j
            self.j = j
            self.q_rows = slice(j * w, (j + 1) * w)
            self.key_rows = [rows(jnp.maximum(n - 1, 0)), rows(n), rows(jnp.minimum(n + 1, nb - 1))]
            self.prev_bound = jnp.where(n > 0, 0, never)
            self.next_bound = jnp.where(n < nb - 1, 0, never)
            self.v_ones = [[jnp.concatenate([v_ref[0, r, h * d:(h + 1) * d], ones_pad], axis=1)
                            for r in self.key_rows] for h in range(ATT_KV_HEADS)]

    def scores(blk, i):
        h = i * pair // grp
        ks = slice(h * d, (h + 1) * d)
        k_band = jnp.concatenate([k_ref[0, r, ks] for r in blk.key_rows], axis=0)
        q_pair = jnp.concatenate([q_ref[0, blk.q_rows, hd * d:(hd + 1) * d]
                                  for hd in range(pair * i, pair * (i + 1))], axis=0)
        q_pair = (q_pair.astype(_F32) * (d ** -0.5 * LOG2E)).astype(_BF16)
        return lax.dot_general(k_band, q_pair, (((1,), (1,)), ((), ())), preferred_element_type=_F32)

    def attend(blk, i, s):
        h = i * pair // grp
        sink = jnp.where(second, sink_ref[pair * i + 1], sink_ref[pair * i]) * LOG2E
        maxes, accs = [], []
        for c in range(3):
            t = s[c * w:(c + 1) * w] + bias_ref[i, c * w:(c + 1) * w, :]
            if c == 0:
                t = jnp.where(key_minus_qry >= blk.prev_bound, t, NEG_INF * LOG2E)
            if c == 2:
                t = jnp.where(-key_minus_qry >= blk.next_bound, t, NEG_INF * LOG2E)
            m = jnp.maximum(jnp.max(t, axis=0, keepdims=True), sink)
            p = jnp.exp2(t - m)
            maxes.append(m)
            accs.append(lax.dot_general(blk.v_ones[h][c], p.astype(_BF16), (((0,), (0,)), ((), ())),
                                        preferred_element_type=_F32))
        m_all = jnp.maximum(jnp.maximum(maxes[0], maxes[1]), maxes[2])
        weights = [jnp.exp2(m - m_all) for m in maxes]
        total = accs[0] * weights[0] + accs[1] * weights[1] + accs[2] * weights[2]
        return total[0:d] / (total[d:d + 1] + jnp.exp2(sink - m_all))

    blocks = [Block(j) for j in range(ATT_BLOCKS_PER_STEP)]
    work = [(blk, i) for blk in blocks for i in range(n_pairs)]
    pending = [scores(*item) for item in work[:ATT_SCORES_AHEAD]]
    pieces = []

    def make_item(k):
        def run():
            blk, i = work[k]
            if k + ATT_SCORES_AHEAD < len(work):
                pending.append(scores(*work[k + ATT_SCORES_AHEAD]))
            out = attend(blk, i, pending.pop(0))
            pieces.extend(out[:, j * w:(j + 1) * w] for j in range(pair))
            if i == n_pairs - 1:
                write_block(blk.j, jnp.concatenate(pieces, axis=0).T)
                pieces.clear()
        return run

    return [make_item(k) for k in range(len(work))]


def _out_mlp_kernel(m_ref, x_ref, w_out_ref, g_post_mix_ref, g_pre_mlp_ref, w_up_ref, w_down_ref,
                    g_post_mlp_ref, g_next_ref, w_next_ref, o_ref, p_ref):
    n_sub = m_ref.shape[0] // MLP_SUBTILE
    rows = [slice(j * MLP_SUBTILE, (j + 1) * MLP_SUBTILE) for j in range(n_sub)]
    ys = [jnp.dot(m_ref[r, :], w_out_ref[...], preferred_element_type=_F32) for r in rows]
    xs, acts = [], []
    for r, y in zip(rows, ys):
        x = x_ref[r, :] + _rms(y, g_post_mix_ref[...])
        h = _rms(x, g_pre_mlp_ref[...]).astype(_BF16)
        xs.append(x)
        acts.append(jnp.dot(h, w_up_ref[...], preferred_element_type=_F32))
    zs = []
    for a in acts:
        a = jnp.maximum(a, 0.0)
        zs.append(jnp.dot((a * a).astype(_BF16), w_down_ref[...], preferred_element_type=_F32))
    outs = []
    for r, x, z in zip(rows, xs, zs):
        outs.append(x + _rms(z, g_post_mlp_ref[...]))
        o_ref[r, :] = outs[-1]
    for r, out in zip(rows, outs):
        h = _rms(out, g_next_ref[...]).astype(_BF16)
        p_ref[r, :] = jnp.dot(h, w_next_ref[...], preferred_element_type=_F32).astype(p_ref.dtype)


def _out_mlp(mixed, x2d, layer, w_out, g_post_mix, g_pre_mlp, w_up, w_down, g_post_mlp, g_pre_mix, w_next, jobs=()):
    t, d = x2d.shape
    dm = mixed.shape[1]
    n_next = w_next.shape[2]
    n_steps = t // MLP_TILE
    job_in, job_out, job_shapes = _convert_specs(jobs, n_steps, lambda i: i)
    out = pl.pallas_call(
        _with_converts(_out_mlp_kernel, 10, 2, len(jobs)),
        grid=(n_steps,),
        in_specs=[
            pl.BlockSpec((MLP_TILE, dm), lambda i: (i, 0)),
            pl.BlockSpec((MLP_TILE, d), lambda i: (i, 0)),
            _resident(w_out, 0),
            _resident(g_post_mix, layer),
            _resident(g_pre_mlp, layer),
            _resident(w_up, 0),
            _resident(w_down, 0),
            _resident(g_post_mlp, layer),
            _resident(g_pre_mix, layer + 1),
            _resident(w_next, 0),
        ] + job_in,
        out_specs=[pl.BlockSpec((MLP_TILE, d), lambda i: (i, 0)),
                   pl.BlockSpec((MLP_TILE, n_next), lambda i: (i, 0))] + job_out,
        out_shape=[jax.ShapeDtypeStruct((t, d), _F32), jax.ShapeDtypeStruct((t, n_next), _BF16)] + job_shapes,
        compiler_params=pltpu.CompilerParams(
            dimension_semantics=("arbitrary",), vmem_limit_bytes=VMEM_LIMIT_BYTES),
        name="out_mlp",
    )(mixed, x2d, w_out, g_post_mix, g_pre_mlp, w_up, w_down, g_post_mlp, g_pre_mix, w_next,
      *(w_f32 for w_f32, _ in jobs))
    return out[0], out[1], out[2:]


def _attn_mlp_kernel(sink_ref, q_ref, k_ref, v_ref, x_ref, w_out_ref, g_post_mix_ref, g_pre_mlp_ref,
                     w_up_ref, w_down_ref, g_post_mlp_ref, o_ref, bias_ref, mixed_ref):
    t = pl.program_id(0)
    tiles_per_batch = k_ref.shape[1] // ATT_TILE
    n_tiles = pl.num_programs(0) - 1
    tile = jnp.minimum(t, n_tiles - 1)
    slot = t % 2

    @pl.when(t == 0)
    def _init():
        _fill_alibi_bias(bias_ref)
        mixed_ref[1] = jnp.zeros_like(mixed_ref[1])

    def write_block(j, value):
        mixed_ref[slot, j * WINDOW:(j + 1) * WINDOW, :] = value.astype(mixed_ref.dtype)

    items = _attention_items(sink_ref, q_ref, k_ref, v_ref, bias_ref,
                             (tile % tiles_per_batch) * ATT_BLOCKS_PER_STEP, write_block)

    d_ff = w_up_ref.shape[1]
    d = w_down_ref.shape[1]
    n_sub = ATT_TILE // MLP_SUBTILE
    sub_rows = [slice(j * MLP_SUBTILE, (j + 1) * MLP_SUBTILE) for j in range(n_sub)]
    vals = [{} for _ in range(n_sub)]

    def out_proj(j):
        def run():
            vals[j]["y"] = jnp.dot(mixed_ref[1 - slot, sub_rows[j], :], w_out_ref[...], preferred_element_type=_F32)
        return run

    def up(j, c):
        def run():
            v = vals[j]
            if c == 0:
                v["x"] = x_ref[sub_rows[j], :] + _rms(v["y"], g_post_mix_ref[...])
                v["h"] = _rms(v["x"], g_pre_mlp_ref[...]).astype(_BF16)
                v["a"] = []
            cols = slice(c * MLP_UP_CHUNK, (c + 1) * MLP_UP_CHUNK)
            a = jnp.maximum(jnp.dot(v["h"], w_up_ref[:, cols], preferred_element_type=_F32), 0.0)
            v["a"].append((a * a).astype(_BF16))
        return run

    def down(j, c):
        def run():
            v = vals[j]
            if c == 0:
                v["act"] = jnp.concatenate(v["a"], axis=1)
                v["z"] = []
            cols = slice(c * MLP_DOWN_CHUNK, (c + 1) * MLP_DOWN_CHUNK)
            v["z"].append(jnp.dot(v["act"], w_down_ref[:, cols], preferred_element_type=_F32))
        return run

    def finish(j):
        def run():
            z = jnp.concatenate(vals[j]["z"], axis=1)
            o_ref[sub_rows[j], :] = vals[j]["x"] + _rms(z, g_post_mlp_ref[...])
        return run

    n_up, n_down = d_ff // MLP_UP_CHUNK, d // MLP_DOWN_CHUNK
    pieces = ([(out_proj(j), w_out_ref.shape[0] * d) for j in range(n_sub)]
              + [(up(j, c), d * MLP_UP_CHUNK) for j in range(n_sub) for c in range(n_up)])
    for j in range(n_sub):
        pieces += [(down(j, c), d_ff * MLP_DOWN_CHUNK) for c in range(n_down)]
        if j + 1 < n_sub:
            pieces.append((finish(j), 0))
    spread, tail = items[:len(items) - ATT_TAIL_ITEMS], items[len(items) - ATT_TAIL_ITEMS:]
    total, done, acc = sum(weight for _, weight in pieces), 0, 0
    for piece, weight in pieces:
        piece()
        acc += weight
        target = min(len(spread), -(-acc * len(spread) // total))
        for item in spread[done:target]:
            item()
        done = target
    for item in tail[:len(tail) // 2]:
        item()
    finish(n_sub - 1)()
    for item in tail[len(tail) // 2:]:
        item()


def _attn_mlp(proj, sink, x2d, layer, w_out, g_post_mix, g_pre_mlp, w_up, w_down, g_post_mlp, jobs=()):
    b, s, _ = proj.shape
    t, d = x2d.shape
    nq = ATT_Q_HEADS * ATT_HEAD_DIM
    nk = ATT_KV_HEADS * ATT_HEAD_DIM
    tiles_per_batch = s // ATT_TILE
    n_tiles = b * tiles_per_batch

    def att_tile(i):
        return jnp.minimum(i, n_tiles - 1)

    def mlp_tile(i):
        return jnp.maximum(i - 1, 0)

    job_in, job_out, job_shapes = _convert_specs(jobs, n_tiles, att_tile)
    out = pl.pallas_call(
        _with_converts(_attn_mlp_kernel, 11, 1, len(jobs)),
        grid=(n_tiles + 1,),
        in_specs=[
            pl.BlockSpec(memory_space=pltpu.SMEM),
            pl.BlockSpec((1, ATT_TILE, nq),
                         lambda i: (att_tile(i) // tiles_per_batch, att_tile(i) % tiles_per_batch, 0)),
            pl.BlockSpec((1, s, nk), lambda i: (att_tile(i) // tiles_per_batch, 0, nq // nk)),
            pl.BlockSpec((1, s, nk), lambda i: (att_tile(i) // tiles_per_batch, 0, nq // nk + 1)),
            pl.BlockSpec((ATT_TILE, d), lambda i: (mlp_tile(i), 0)),
            _resident(w_out, 0),
            _resident(g_post_mix, layer),
            _resident(g_pre_mlp, layer),
            _resident(w_up, 0),
            _resident(w_down, 0),
            _resident(g_post_mlp, layer),
        ] + job_in,
        out_specs=[pl.BlockSpec((ATT_TILE, d), lambda i: (mlp_tile(i), 0))] + job_out,
        out_shape=[jax.ShapeDtypeStruct((t, d), _F32)] + job_shapes,
        scratch_shapes=[
            pltpu.VMEM((ATT_Q_HEADS // ATT_PAIR, 3 * WINDOW, ATT_PAIR * WINDOW), _F32),
            pltpu.VMEM((2, ATT_TILE, nq), _BF16),
        ],
        compiler_params=pltpu.CompilerParams(
            dimension_semantics=("arbitrary",), vmem_limit_bytes=VMEM_LIMIT_BYTES),
        name="attn_mlp",
    )(sink, proj, proj, proj, x2d, w_out, g_post_mix, g_pre_mlp, w_up, w_down, g_post_mlp,
      *(w_f32 for w_f32, _ in jobs))
    return out[0], out[1:]


def kernel(x, w_in_ret, w_out_ret, log_decay_fwd, log_decay_bwd, w_in_attn, w_out_attn, sink_logits,
           g_pre_mix, g_post_mix, g_pre_mlp, g_post_mlp, w_up, w_down):
    b, s, d = x.shape
    x2d = x.reshape(b * s, d)
    g_pre_mix, g_post_mix, g_pre_mlp, g_post_mlp = (
        g.reshape(DEPTH, 1, d) for g in (g_pre_mix, g_post_mix, g_pre_mlp, g_post_mlp))
    w_in = w_in_ret[0:1].astype(_BF16)
    for layer in range(0, DEPTH, 2):
        i = layer // 2
        proj, (w_out, w_up_a, w_down_a) = _norm_proj(
            x2d, g_pre_mix, layer, w_in, 0, jobs=[(w_out_ret, i), (w_up, layer), (w_down, layer)])
        mixed, (w_in_b, w_out_b) = _retention_core(
            proj.reshape(b, s, -1), log_decay_fwd[i], log_decay_bwd[i], jobs=[(w_in_attn, i), (w_out_attn, i)])
        x2d, proj, (w_up_b, w_down_b) = _out_mlp(
            mixed.reshape(b * s, -1), x2d, layer, w_out, g_post_mix, g_pre_mlp, w_up_a, w_down_a, g_post_mlp,
            g_pre_mix, w_in_b, jobs=[(w_up, layer + 1), (w_down, layer + 1)])
        more = layer + 2 < DEPTH
        x2d, next_w_in = _attn_mlp(
            proj.reshape(b, s, -1), sink_logits[i], x2d, layer + 1, w_out_b, g_post_mix, g_pre_mlp, w_up_b,
            w_down_b, g_post_mlp, jobs=[(w_in_ret, i + 1)] if more else [])
        if more:
            (w_in,) = next_w_in
    return x2d.reshape(b, s, d)
```

```python
import jax
import jax.numpy as jnp
from jax import lax
from jax.experimental import pallas as pl
from jax.experimental.pallas import tpu as pltpu

D_MODEL = 1024
DEPTH = 4
RET_HEADS = 4
RET_QK_DIM = 256
RET_V_DIM = 512
ATT_Q_HEADS = 16
ATT_KV_HEADS = 4
ATT_GROUP = ATT_Q_HEADS // ATT_KV_HEADS
ATT_HEAD_DIM = 64
WINDOW = 128
D_FF = 4 * D_MODEL
EPS = 1e-6
NEG_INF = -1e30
LOG2E = 1.4426950408889634

TOKEN_TILE = 1024
PROJ_SUBTILE = 256
MLP_TILE = 512
MLP_SUBTILE = 256
RET_CHUNK = 256
RET_SCORES_AHEAD = 4
ATT_PAIR = 2
ATT_ONES_PAD = 16
ATT_BLOCKS_PER_STEP = 4
ATT_TILE = ATT_BLOCKS_PER_STEP * WINDOW
ATT_SCORES_AHEAD = 2
ATT_TAIL_ITEMS = 2
MLP_UP_CHUNK = 512
MLP_DOWN_CHUNK = 256
VMEM_LIMIT_BYTES = 56 * 1024 * 1024

_BF16 = jnp.bfloat16
_F32 = jnp.float32


def _rms(x, g):
    y = x * lax.rsqrt(jnp.mean(x * x, axis=-1, keepdims=True) + EPS)
    return y * g


def _resident(stacked, index):
    zeros = (0,) * (stacked.ndim - 1)
    return pl.BlockSpec((None,) + stacked.shape[1:], lambda *_: (index,) + zeros, pipeline_mode=pl.Buffered(1))


def _convert_specs(jobs, n_steps, step_of):
    in_specs, out_specs, out_shapes = [], [], []
    for w, index in jobs:
        _, k, n = w.shape
        rows = k // n_steps
        assert rows * n_steps == k and rows % 16 == 0, (w.shape, n_steps)
        in_specs.append(pl.BlockSpec((None, rows, n), lambda *g, index=index: (index, step_of(*g), 0)))
        out_specs.append(pl.BlockSpec((None, rows, n), lambda *g: (0, step_of(*g), 0)))
        out_shapes.append(jax.ShapeDtypeStruct((1, k, n), _BF16))
    return in_specs, out_specs, out_shapes


def _with_converts(body, n_in, n_out, n_jobs):
    def kernel(*refs):
        ins, rest = refs[:n_in], refs[n_in:]
        job_ins, rest = rest[:n_jobs], rest[n_jobs:]
        outs, rest = rest[:n_out], rest[n_out:]
        job_outs, scratch = rest[:n_jobs], rest[n_jobs:]
        body(*ins, *outs, *scratch)
        for src, dst in zip(job_ins, job_outs):
            dst[...] = src[...].astype(dst.dtype)
    return kernel


def _norm_proj_kernel(x_ref, g_ref, w_ref, o_ref):
    for j in range(x_ref.shape[0] // PROJ_SUBTILE):
        r = slice(j * PROJ_SUBTILE, (j + 1) * PROJ_SUBTILE)
        h = _rms(x_ref[r, :], g_ref[...]).astype(_BF16)
        o_ref[r, :] = jnp.dot(h, w_ref[...], preferred_element_type=_F32).astype(o_ref.dtype)


def _norm_proj(x2d, g, layer, w, w_index, jobs=()):
    t, d = x2d.shape
    n = w.shape[2]
    n_steps = t // TOKEN_TILE
    job_in, job_out, job_shapes = _convert_specs(jobs, n_steps, lambda i: i)
    out = pl.pallas_call(
        _with_converts(_norm_proj_kernel, 3, 1, len(jobs)),
        grid=(n_steps,),
        in_specs=[
            pl.BlockSpec((TOKEN_TILE, d), lambda i: (i, 0)),
            _resident(g, layer),
            _resident(w, w_index),
        ] + job_in,
        out_specs=[pl.BlockSpec((TOKEN_TILE, n), lambda i: (i, 0))] + job_out,
        out_shape=[jax.ShapeDtypeStruct((t, n), _BF16)] + job_shapes,
        compiler_params=pltpu.CompilerParams(
            dimension_semantics=("arbitrary",), vmem_limit_bytes=VMEM_LIMIT_BYTES),
        name="norm_proj",
    )(x2d, g, w, *(w_f32 for w_f32, _ in jobs))
    return out[0], out[1:]


def _retention_kernel(lgf_ref, lgb_ref, q_ref, k_ref, v_ref, g_ref, o_ref, rhs_ref, state_ref):
    c = RET_CHUNK
    dk = RET_QK_DIM
    seq = q_ref.shape[1]
    n_chunks = seq // c
    head = pl.program_id(1)
    lf = lgf_ref[head]
    lb = lgb_ref[head]
    k_scale = dk ** -0.5

    row = lax.broadcasted_iota(jnp.int32, (c, c), 0)
    col = lax.broadcasted_iota(jnp.int32, (c, c), 1)
    diff = (row - col).astype(_F32)
    inner_decay = jnp.where(
        diff >= 0.0,
        jnp.exp(lf * jnp.maximum(diff, 0.0)),
        jnp.exp(lb * jnp.maximum(-diff, 0.0))) * k_scale
    idx = lax.broadcasted_iota(jnp.int32, (c, dk), 0).astype(_F32)
    q_decay_f = jnp.exp(lf * (idx + 1.0)).astype(_BF16)
    k_decay_f = (jnp.exp(lf * (c - 1.0 - idx)) * k_scale).astype(_BF16)
    q_decay_b = jnp.exp(lb * (c - idx)).astype(_BF16)
    k_decay_b = (jnp.exp(lb * idx) * k_scale).astype(_BF16)
    chunk_decay_f = jnp.exp(lf * c)
    chunk_decay_b = jnp.exp(lb * c)

    def rows(i):
        return slice(i * c, (i + 1) * c)

    def kv_product(i, k_decay):
        return lax.dot_general(k_ref[0, rows(i), :] * k_decay, v_ref[0, rows(i), :], (((0,), (0,)), ((), ())),
                               preferred_element_type=_F32)

    state_ref[...] = jnp.zeros_like(state_ref)
    for i in range(n_chunks):
        rhs_ref[i, 0:c, :] = v_ref[0, rows(i), :]
        rhs_ref[i, c:c + dk, :] = state_ref[...].astype(_BF16)
        if i + 1 < n_chunks:
            state_ref[...] = state_ref[...] * chunk_decay_f + kv_product(i, k_decay_f)
    state_ref[...] = jnp.zeros_like(state_ref)
    for i in reversed(range(n_chunks)):
        rhs_ref[i, c + dk:c + 2 * dk, :] = state_ref[...].astype(_BF16)
        if i > 0:
            state_ref[...] = state_ref[...] * chunk_decay_b + kv_product(i, k_decay_b)

    def chunk_scores(i):
        return lax.dot_general(q_ref[0, rows(i), :], k_ref[0, rows(i), :], (((1,), (1,)), ((), ())),
                               preferred_element_type=_F32)

    pending = [chunk_scores(i) for i in range(min(RET_SCORES_AHEAD, n_chunks))]
    for i in range(n_chunks):
        if i + RET_SCORES_AHEAD < n_chunks:
            pending.append(chunk_scores(i + RET_SCORES_AHEAD))
        s = pending.pop(0)
        q = q_ref[0, rows(i), :]
        lhs = jnp.concatenate([(s * inner_decay).astype(_BF16), q * q_decay_f, q * q_decay_b], axis=1)
        o = jnp.dot(lhs, rhs_ref[i], preferred_element_type=_F32)
        o = o * lax.rsqrt(jnp.mean(o * o, axis=-1, keepdims=True) + EPS)
        half = g_ref[0, rows(i), :].astype(_F32) * 0.5
        silu = half + half * jnp.tanh(half)
        o_ref[0, rows(i), :] = (silu * o).astype(o_ref.dtype)


def _retention_core(proj, log_decay_fwd, log_decay_bwd, jobs=()):
    b, s, _ = proj.shape
    dk, dv, h = RET_QK_DIM, RET_V_DIM, RET_HEADS
    k_off = h * dk // dk
    v_off = 2 * h * dk // dv
    g_off = (2 * h * dk + h * dv) // dv
    smem = pl.BlockSpec(memory_space=pltpu.SMEM)
    job_in, job_out, job_shapes = _convert_specs(jobs, b * h, lambda bi, hi: bi * h + hi)
    out = pl.pallas_call(
        _with_converts(_retention_kernel, 6, 1, len(jobs)),
        grid=(b, h),
        in_specs=[
            smem, smem,
            pl.BlockSpec((1, s, dk), lambda bi, hi: (bi, 0, hi)),
            pl.BlockSpec((1, s, dk), lambda bi, hi: (bi, 0, k_off + hi)),
            pl.BlockSpec((1, s, dv), lambda bi, hi: (bi, 0, v_off + hi)),
            pl.BlockSpec((1, s, dv), lambda bi, hi: (bi, 0, g_off + hi)),
        ] + job_in,
        out_specs=[pl.BlockSpec((1, s, dv), lambda bi, hi: (bi, 0, hi))] + job_out,
        out_shape=[jax.ShapeDtypeStruct((b, s, h * dv), _BF16)] + job_shapes,
        scratch_shapes=[
            pltpu.VMEM((s // RET_CHUNK, RET_CHUNK + 2 * dk, dv), _BF16),
            pltpu.VMEM((dk, dv), _F32),
        ],
        compiler_params=pltpu.CompilerParams(
            dimension_semantics=("arbitrary", "arbitrary"), vmem_limit_bytes=VMEM_LIMIT_BYTES),
        name="retention_core",
    )(log_decay_fwd, log_decay_bwd, proj, proj, proj, proj, *(w_f32 for w_f32, _ in jobs))
    return out[0], out[1:]


def _fill_alibi_bias(bias_ref):
    w, pair = WINDOW, ATT_PAIR
    second = lax.broadcasted_iota(jnp.int32, (1, pair * w), 1) >= w
    key = lax.broadcasted_iota(jnp.int32, (3 * w, pair * w), 0)
    qry = lax.broadcasted_iota(jnp.int32, (3 * w, pair * w), 1) & (w - 1)
    dist = jnp.abs(w + qry - key).astype(_F32)
    for i in range(ATT_Q_HEADS // pair):
        slope = [2.0 ** (-8.0 * (pair * i + j + 1) / ATT_Q_HEADS) for j in range(pair)]
        bias_ref[i] = (-jnp.where(second, slope[1], slope[0]) * dist) * LOG2E


def _attention_items(sink_ref, q_ref, k_ref, v_ref, bias_ref, first_block, write_block):
    w = WINDOW
    d = ATT_HEAD_DIM
    grp = ATT_GROUP
    pair = ATT_PAIR
    n_pairs = ATT_Q_HEADS // pair
    nb = k_ref.shape[1] // w

    lane_qry = lax.broadcasted_iota(jnp.int32, (1, pair * w), 1)
    second = lane_qry >= w

    def rows(i):
        return pl.ds(pl.multiple_of(i * w, w), w)

    key_minus_qry = lax.broadcasted_iota(jnp.int32, (w, pair * w), 0) - (lane_qry & (w - 1))
    never = 2 * w
    ones_pad = jnp.ones((w, ATT_ONES_PAD), _BF16)

    class Block:
        def __init__(self, j):
            n = first_block + j
            self.j = j
            self.q_rows = slice(j * w, (j + 1) * w)
            self.key_rows = [rows(jnp.maximum(n - 1, 0)), rows(n), rows(jnp.minimum(n + 1, nb - 1))]
            self.prev_bound = jnp.where(n > 0, 0, never)
            self.next_bound = jnp.where(n < nb - 1, 0, never)
            self.v_ones = [[jnp.concatenate([v_ref[0, r, h * d:(h + 1) * d], ones_pad], axis=1)
                            for r in self.key_rows] for h in range(ATT_KV_HEADS)]

    def scores(blk, i):
        h = i * pair // grp
        ks = slice(h * d, (h + 1) * d)
        k_band = jnp.concatenate([k_ref[0, r, ks] for r in blk.key_rows], axis=0)
        q_pair = jnp.concatenate([q_ref[0, blk.q_rows, hd * d:(hd + 1) * d]
                                  for hd in range(pair * i, pair * (i + 1))], axis=0)
        q_pair = (q_pair.astype(_F32) * (d ** -0.5 * LOG2E)).astype(_BF16)
        return lax.dot_general(k_band, q_pair, (((1,), (1,)), ((), ())), preferred_element_type=_F32)

    def attend(blk, i, s):
        h = i * pair // grp
        sink = jnp.where(second, sink_ref[pair * i + 1], sink_ref[pair * i]) * LOG2E
        maxes, accs = [], []
        for c in range(3):
            t = s[c * w:(c + 1) * w] + bias_ref[i, c * w:(c + 1) * w, :]
            if c == 0:
                t = jnp.where(key_minus_qry >= blk.prev_bound, t, NEG_INF * LOG2E)
            if c == 2:
                t = jnp.where(-key_minus_qry >= blk.next_bound, t, NEG_INF * LOG2E)
            m = jnp.maximum(jnp.max(t, axis=0, keepdims=True), sink)
            p = jnp.exp2(t - m)
            maxes.append(m)
            accs.append(lax.dot_general(blk.v_ones[h][c], p.astype(_BF16), (((0,), (0,)), ((), ())),
                                        preferred_element_type=_F32))
        m_all = jnp.maximum(jnp.maximum(maxes[0], maxes[1]), maxes[2])
        weights = [jnp.exp2(m - m_all) for m in maxes]
        total = accs[0] * weights[0] + accs[1] * weights[1] + accs[2] * weights[2]
        return total[0:d] / (total[d:d + 1] + jnp.exp2(sink - m_all))

    blocks = [Block(j) for j in range(ATT_BLOCKS_PER_STEP)]
    work = [(blk, i) for blk in blocks for i in range(n_pairs)]
    pending = [scores(*item) for item in work[:ATT_SCORES_AHEAD]]
    pieces = []

    def make_item(k):
        def run():
            blk, i = work[k]
            if k + ATT_SCORES_AHEAD < len(work):
                pending.append(scores(*work[k + ATT_SCORES_AHEAD]))
            out = attend(blk, i, pending.pop(0))
            pieces.extend(out[:, j * w:(j + 1) * w] for j in range(pair))
            if i == n_pairs - 1:
                write_block(blk.j, jnp.concatenate(pieces, axis=0).T)
                pieces.clear()
        return run

    return [make_item(k) for k in range(len(work))]


def _out_mlp_kernel(m_ref, x_ref, w_out_ref, g_post_mix_ref, g_pre_mlp_ref, w_up_ref, w_down_ref,
                    g_post_mlp_ref, g_next_ref, w_next_ref, o_ref, p_ref):
    n_sub = m_ref.shape[0] // MLP_SUBTILE
    rows = [slice(j * MLP_SUBTILE, (j + 1) * MLP_SUBTILE) for j in range(n_sub)]
    ys = [jnp.dot(m_ref[r, :], w_out_ref[...], preferred_element_type=_F32) for r in rows]
    xs, acts = [], []
    for r, y in zip(rows, ys):
        x = x_ref[r, :] + _rms(y, g_post_mix_ref[...])
        h = _rms(x, g_pre_mlp_ref[...]).astype(_BF16)
        xs.append(x)
        acts.append(jnp.dot(h, w_up_ref[...], preferred_element_type=_F32))
    zs = []
    for a in acts:
        a = jnp.maximum(a, 0.0)
        zs.append(jnp.dot((a * a).astype(_BF16), w_down_ref[...], preferred_element_type=_F32))
    outs = []
    for r, x, z in zip(rows, xs, zs):
        outs.append(x + _rms(z, g_post_mlp_ref[...]))
        o_ref[r, :] = outs[-1]
    for r, out in zip(rows, outs):
        h = _rms(out, g_next_ref[...]).astype(_BF16)
        p_ref[r, :] = jnp.dot(h, w_next_ref[...], preferred_element_type=_F32).astype(p_ref.dtype)


def _out_mlp(mixed, x2d, layer, w_out, g_post_mix, g_pre_mlp, w_up, w_down, g_post_mlp, g_pre_mix, w_next, jobs=()):
    t, d = x2d.shape
    dm = mixed.shape[1]
    n_next = w_next.shape[2]
    n_steps = t // MLP_TILE
    job_in, job_out, job_shapes = _convert_specs(jobs, n_steps, lambda i: i)
    out = pl.pallas_call(
        _with_converts(_out_mlp_kernel, 10, 2, len(jobs)),
        grid=(n_steps,),
        in_specs=[
            pl.BlockSpec((MLP_TILE, dm), lambda i: (i, 0)),
            pl.BlockSpec((MLP_TILE, d), lambda i: (i, 0)),
            _resident(w_out, 0),
            _resident(g_post_mix, layer),
            _resident(g_pre_mlp, layer),
            _resident(w_up, 0),
            _resident(w_down, 0),
            _resident(g_post_mlp, layer),
            _resident(g_pre_mix, layer + 1),
            _resident(w_next, 0),
        ] + job_in,
        out_specs=[pl.BlockSpec((MLP_TILE, d), lambda i: (i, 0)),
                   pl.BlockSpec((MLP_TILE, n_next), lambda i: (i, 0))] + job_out,
        out_shape=[jax.ShapeDtypeStruct((t, d), _F32), jax.ShapeDtypeStruct((t, n_next), _BF16)] + job_shapes,
        compiler_params=pltpu.CompilerParams(
            dimension_semantics=("arbitrary",), vmem_limit_bytes=VMEM_LIMIT_BYTES),
        name="out_mlp",
    )(mixed, x2d, w_out, g_post_mix, g_pre_mlp, w_up, w_down, g_post_mlp, g_pre_mix, w_next,
      *(w_f32 for w_f32, _ in jobs))
    return out[0], out[1], out[2:]


def _attn_mlp_kernel(sink_ref, q_ref, k_ref, v_ref, x_ref, w_out_ref, g_post_mix_ref, g_pre_mlp_ref,
                     w_up_ref, w_down_ref, g_post_mlp_ref, o_ref, bias_ref, mixed_ref):
    t = pl.program_id(0)
    tiles_per_batch = k_ref.shape[1] // ATT_TILE
    n_tiles = pl.num_programs(0) - 1
    tile = jnp.minimum(t, n_tiles - 1)
    slot = t % 2

    @pl.when(t == 0)
    def _init():
        _fill_alibi_bias(bias_ref)
        mixed_ref[1] = jnp.zeros_like(mixed_ref[1])

    def write_block(j, value):
        mixed_ref[slot, j * WINDOW:(j + 1) * WINDOW, :] = value.astype(mixed_ref.dtype)

    items = _attention_items(sink_ref, q_ref, k_ref, v_ref, bias_ref,
                             (tile % tiles_per_batch) * ATT_BLOCKS_PER_STEP, write_block)

    d_ff = w_up_ref.shape[1]
    d = w_down_ref.shape[1]
    n_sub = ATT_TILE // MLP_SUBTILE
    sub_rows = [slice(j * MLP_SUBTILE, (j + 1) * MLP_SUBTILE) for j in range(n_sub)]
    vals = [{} for _ in range(n_sub)]

    def out_proj(j):
        def run():
            vals[j]["y"] = jnp.dot(mixed_ref[1 - slot, sub_rows[j], :], w_out_ref[...], preferred_element_type=_F32)
        return run

    def up(j, c):
        def run():
            v = vals[j]
            if c == 0:
                v["x"] = x_ref[sub_rows[j], :] + _rms(v["y"], g_post_mix_ref[...])
                v["h"] = _rms(v["x"], g_pre_mlp_ref[...]).astype(_BF16)
                v["a"] = []
            cols = slice(c * MLP_UP_CHUNK, (c + 1) * MLP_UP_CHUNK)
            a = jnp.maximum(jnp.dot(v["h"], w_up_ref[:, cols], preferred_element_type=_F32), 0.0)
            v["a"].append((a * a).astype(_BF16))
        return run

    def down(j, c):
        def run():
            v = vals[j]
            if c == 0:
                v["act"] = jnp.concatenate(v["a"], axis=1)
                v["z"] = []
            cols = slice(c * MLP_DOWN_CHUNK, (c + 1) * MLP_DOWN_CHUNK)
            v["z"].append(jnp.dot(v["act"], w_down_ref[:, cols], preferred_element_type=_F32))
        return run

    def finish(j):
        def run():
            z = jnp.concatenate(vals[j]["z"], axis=1)
            o_ref[sub_rows[j], :] = vals[j]["x"] + _rms(z, g_post_mlp_ref[...])
        return run

    n_up, n_down = d_ff // MLP_UP_CHUNK, d // MLP_DOWN_CHUNK
    pieces = ([(out_proj(j), w_out_ref.shape[0] * d) for j in range(n_sub)]
              + [(up(j, c), d * MLP_UP_CHUNK) for j in range(n_sub) for c in range(n_up)])
    for j in range(n_sub):
        pieces += [(down(j, c), d_ff * MLP_DOWN_CHUNK) for c in range(n_down)]
        if j + 1 < n_sub:
            pieces.append((finish(j), 0))
    spread, tail = items[:len(items) - ATT_TAIL_ITEMS], items[len(items) - ATT_TAIL_ITEMS:]
    total, done, acc = sum(weight for _, weight in pieces), 0, 0
    for piece, weight in pieces:
        piece()
        acc += weight
        target = min(len(spread), -(-acc * len(spread) // total))
        for item in spread[done:target]:
            item()
        done = target
    for item in tail[:len(tail) // 2]:
        item()
    finish(n_sub - 1)()
    for item in tail[len(tail) // 2:]:
        item()


def _attn_mlp(proj, sink, x2d, layer, w_out, g_post_mix, g_pre_mlp, w_up, w_down, g_post_mlp, jobs=()):
    b, s, _ = proj.shape
    t, d = x2d.shape
    nq = ATT_Q_HEADS * ATT_HEAD_DIM
    nk = ATT_KV_HEADS * ATT_HEAD_DIM
    tiles_per_batch = s // ATT_TILE
    n_tiles = b * tiles_per_batch

    def att_tile(i):
        return jnp.minimum(i, n_tiles - 1)

    def mlp_tile(i):
        return jnp.maximum(i - 1, 0)

    job_in, job_out, job_shapes = _convert_specs(jobs, n_tiles, att_tile)
    out = pl.pallas_call(
        _with_converts(_attn_mlp_kernel, 11, 1, len(jobs)),
        grid=(n_tiles + 1,),
        in_specs=[
            pl.BlockSpec(memory_space=pltpu.SMEM),
            pl.BlockSpec((1, ATT_TILE, nq),
                         lambda i: (att_tile(i) // tiles_per_batch, att_tile(i) % tiles_per_batch, 0)),
            pl.BlockSpec((1, s, nk), lambda i: (att_tile(i) // tiles_per_batch, 0, nq // nk)),
            pl.BlockSpec((1, s, nk), lambda i: (att_tile(i) // tiles_per_batch, 0, nq // nk + 1)),
            pl.BlockSpec((ATT_TILE, d), lambda i: (mlp_tile(i), 0)),
            _resident(w_out, 0),
            _resident(g_post_mix, layer),
            _resident(g_pre_mlp, layer),
            _resident(w_up, 0),
            _resident(w_down, 0),
            _resident(g_post_mlp, layer),
        ] + job_in,
        out_specs=[pl.BlockSpec((ATT_TILE, d), lambda i: (mlp_tile(i), 0))] + job_out,
        out_shape=[jax.ShapeDtypeStruct((t, d), _F32)] + job_shapes,
        scratch_shapes=[
            pltpu.VMEM((ATT_Q_HEADS // ATT_PAIR, 3 * WINDOW, ATT_PAIR * WINDOW), _F32),
            pltpu.VMEM((2, ATT_TILE, nq), _BF16),
        ],
        compiler_params=pltpu.CompilerParams(
            dimension_semantics=("arbitrary",), vmem_limit_bytes=VMEM_LIMIT_BYTES),
        name="attn_mlp",
    )(sink, proj, proj, proj, x2d, w_out, g_post_mix, g_pre_mlp, w_up, w_down, g_post_mlp,
      *(w_f32 for w_f32, _ in jobs))
    return out[0], out[1:]


def kernel(x, w_in_ret, w_out_ret, log_decay_fwd, log_decay_bwd, w_in_attn, w_out_attn, sink_logits,
           g_pre_mix, g_post_mix, g_pre_mlp, g_post_mlp, w_up, w_down):
    b, s, d = x.shape
    x2d = x.reshape(b * s, d)
    g_pre_mix, g_post_mix, g_pre_mlp, g_post_mlp = (
        g.reshape(DEPTH, 1, d) for g in (g_pre_mix, g_post_mix, g_pre_mlp, g_post_mlp))
    w_in = w_in_ret[0:1].astype(_BF16)
    for layer in range(0, DEPTH, 2):
        i = layer // 2
        proj, (w_out, w_up_a, w_down_a) = _norm_proj(
            x2d, g_pre_mix, layer, w_in, 0, jobs=[(w_out_ret, i), (w_up, layer), (w_down, layer)])
        mixed, (w_in_b, w_out_b) = _retention_core(
            proj.reshape(b, s, -1), log_decay_fwd[i], log_decay_bwd[i], jobs=[(w_in_attn, i), (w_out_attn, i)])
        x2d, proj, (w_up_b, w_down_b) = _out_mlp(
            mixed.reshape(b * s, -1), x2d, layer, w_out, g_post_mix, g_pre_mlp, w_up_a, w_down_a, g_post_mlp,
            g_pre_mix, w_in_b, jobs=[(w_up, layer + 1), (w_down, layer + 1)])
        more = layer + 2 < DEPTH
        x2d, next_w_in = _attn_mlp(
            proj.reshape(b, s, -1), sink_logits[i], x2d, layer + 1, w_out_b, g_post_mix, g_pre_mlp, w_up_b,
            w_down_b, g_post_mlp, jobs=[(w_in_ret, i + 1)] if more else [])
        if more:
            (w_in,) = next_w_in
    return x2d.reshape(b, s, d)
```

```python
import jax
import jax.numpy as jnp
from jax import lax
from jax.experimental import pallas as pl
from jax.experimental.pallas import tpu as pltpu

D_MODEL = 1024
DEPTH = 4
RET_HEADS = 4
RET_QK_DIM = 256
RET_V_DIM = 512
ATT_Q_HEADS = 16
ATT_KV_HEADS = 4
ATT_GROUP = ATT_Q_HEADS // ATT_KV_HEADS
ATT_HEAD_DIM = 64
WINDOW = 128
D_FF = 4 * D_MODEL
EPS = 1e-6
NEG_INF = -1e30
LOG2E = 1.4426950408889634

TOKEN_TILE = 1024
PROJ_SUBTILE = 256
MLP_TILE = 512
MLP_SUBTILE = 256
RET_CHUNK = 256
RET_SCORES_AHEAD = 4
ATT_PAIR = 2
ATT_ONES_PAD = 16
ATT_BLOCKS_PER_STEP = 4
ATT_TILE = ATT_BLOCKS_PER_STEP * WINDOW
ATT_SCORES_AHEAD = 2
ATT_TAIL_ITEMS = 2
MLP_UP_CHUNK = 512
MLP_DOWN_CHUNK = 256
VMEM_LIMIT_BYTES = 56 * 1024 * 1024

_BF16 = jnp.bfloat16
_F32 = jnp.float32


def _rms(x, g):
    y = x * lax.rsqrt(jnp.mean(x * x, axis=-1, keepdims=True) + EPS)
    return y * g


def _resident(stacked, index):
    zeros = (0,) * (stacked.ndim - 1)
    return pl.BlockSpec((None,) + stacked.shape[1:], lambda *_: (index,) + zeros, pipeline_mode=pl.Buffered(1))


def _convert_specs(jobs, n_steps, step_of):
    in_specs, out_specs, out_shapes = [], [], []
    for w, index in jobs:
        _, k, n = w.shape
        rows = k // n_steps
        assert rows * n_steps == k and rows % 16 == 0, (w.shape, n_steps)
        in_specs.append(pl.BlockSpec((None, rows, n), lambda *g, index=index: (index, step_of(*g), 0)))
        out_specs.append(pl.BlockSpec((None, rows, n), lambda *g: (0, step_of(*g), 0)))
        out_shapes.append(jax.ShapeDtypeStruct((1, k, n), _BF16))
    return in_specs, out_specs, out_shapes


def _with_converts(body, n_in, n_out, n_jobs):
    def kernel(*refs):
        ins, rest = refs[:n_in], refs[n_in:]
        job_ins, rest = rest[:n_jobs], rest[n_jobs:]
        outs, rest = rest[:n_out], rest[n_out:]
        job_outs, scratch = rest[:n_jobs], rest[n_jobs:]
        body(*ins, *outs, *scratch)
        for src, dst in zip(job_ins, job_outs):
            dst[...] = src[...].astype(dst.dtype)
    return kernel


def _norm_proj_kernel(x_ref, g_ref, w_ref, o_ref):
    for j in range(x_ref.shape[0] // PROJ_SUBTILE):
        r = slice(j * PROJ_SUBTILE, (j + 1) * PROJ_SUBTILE)
        h = _rms(x_ref[r, :], g_ref[...]).astype(_BF16)
        o_ref[r, :] = jnp.dot(h, w_ref[...], preferred_element_type=_F32).astype(o_ref.dtype)


def _norm_proj(x2d, g, layer, w, w_index, jobs=()):
    t, d = x2d.shape
    n = w.shape[2]
    n_steps = t // TOKEN_TILE
    job_in, job_out, job_shapes = _convert_specs(jobs, n_steps, lambda i: i)
    out = pl.pallas_call(
        _with_converts(_norm_proj_kernel, 3, 1, len(jobs)),
        grid=(n_steps,),
        in_specs=[
            pl.BlockSpec((TOKEN_TILE, d), lambda i: (i, 0)),
            _resident(g, layer),
            _resident(w, w_index),
        ] + job_in,
        out_specs=[pl.BlockSpec((TOKEN_TILE, n), lambda i: (i, 0))] + job_out,
        out_shape=[jax.ShapeDtypeStruct((t, n), _BF16)] + job_shapes,
        compiler_params=pltpu.CompilerParams(
            dimension_semantics=("arbitrary",), vmem_limit_bytes=VMEM_LIMIT_BYTES),
        name="norm_proj",
    )(x2d, g, w, *(w_f32 for w_f32, _ in jobs))
    return out[0], out[1:]


def _retention_kernel(lgf_ref, lgb_ref, q_ref, k_ref, v_ref, g_ref, o_ref, rhs_ref, state_ref, inner_ref, decay_ref):
    c = RET_CHUNK
    dk = RET_QK_DIM
    seq = q_ref.shape[1]
    n_chunks = seq // c
    head = pl.program_id(1)
    lf = lgf_ref[head]
    lb = lgb_ref[head]
    k_scale = dk ** -0.5

    @pl.when(pl.program_id(0) == 0)
    def _build_tables():
        row = lax.broadcasted_iota(jnp.int32, (c, c), 0)
        col = lax.broadcasted_iota(jnp.int32, (c, c), 1)
        diff = (row - col).astype(_F32)
        inner_ref[head] = jnp.where(
            diff >= 0.0,
            jnp.exp(lf * jnp.maximum(diff, 0.0)),
            jnp.exp(lb * jnp.maximum(-diff, 0.0))) * k_scale
        idx = lax.broadcasted_iota(jnp.int32, (c, dk), 0).astype(_F32)
        decay_ref[head, 0] = jnp.exp(lf * (idx + 1.0)).astype(_BF16)
        decay_ref[head, 1] = (jnp.exp(lf * (c - 1.0 - idx)) * k_scale).astype(_BF16)
        decay_ref[head, 2] = jnp.exp(lb * (c - idx)).astype(_BF16)
        decay_ref[head, 3] = (jnp.exp(lb * idx) * k_scale).astype(_BF16)

    q_decay_f, k_decay_f, q_decay_b, k_decay_b = (decay_ref.at[head, j] for j in range(4))
    chunk_decay_f = jnp.exp(lf * c)
    chunk_decay_b = jnp.exp(lb * c)

    def rows(i):
        return slice(i * c, (i + 1) * c)

    def kv_product(i, k_decay):
        return lax.dot_general(k_ref[0, rows(i), :] * k_decay[...], v_ref[0, rows(i), :], (((0,), (0,)), ((), ())),
                               preferred_element_type=_F32)

    state_ref[...] = jnp.zeros_like(state_ref)
    for i in range(n_chunks):
        rhs_ref[i, 0:c, :] = v_ref[0, rows(i), :]
        rhs_ref[i, c:c + dk, :] = state_ref[...].astype(_BF16)
        if i + 1 < n_chunks:
            state_ref[...] = state_ref[...] * chunk_decay_f + kv_product(i, k_decay_f)
    state_ref[...] = jnp.zeros_like(state_ref)
    for i in reversed(range(n_chunks)):
        rhs_ref[i, c + dk:c + 2 * dk, :] = state_ref[...].astype(_BF16)
        if i > 0:
            state_ref[...] = state_ref[...] * chunk_decay_b + kv_product(i, k_decay_b)

    def chunk_scores(i):
        return lax.dot_general(q_ref[0, rows(i), :], k_ref[0, rows(i), :], (((1,), (1,)), ((), ())),
                               preferred_element_type=_F32)

    pending = [chunk_scores(i) for i in range(min(RET_SCORES_AHEAD, n_chunks))]
    for i in range(n_chunks):
        if i + RET_SCORES_AHEAD < n_chunks:
            pending.append(chunk_scores(i + RET_SCORES_AHEAD))
        s = pending.pop(0)
        q = q_ref[0, rows(i), :]
        lhs = jnp.concatenate([(s * inner_ref[head]).astype(_BF16), q * q_decay_f[...], q * q_decay_b[...]], axis=1)
        o = jnp.dot(lhs, rhs_ref[i], preferred_element_type=_F32)
        o = o * lax.rsqrt(jnp.mean(o * o, axis=-1, keepdims=True) + EPS)
        half = g_ref[0, rows(i), :].astype(_F32) * 0.5
        silu = half + half * jnp.tanh(half)
        o_ref[0, rows(i), :] = (silu * o).astype(o_ref.dtype)


def _retention_core(proj, log_decay_fwd, log_decay_bwd, jobs=()):
    b, s, _ = proj.shape
    dk, dv, h = RET_QK_DIM, RET_V_DIM, RET_HEADS
    k_off = h * dk // dk
    v_off = 2 * h * dk // dv
    g_off = (2 * h * dk + h * dv) // dv
    smem = pl.BlockSpec(memory_space=pltpu.SMEM)
    job_in, job_out, job_shapes = _convert_specs(jobs, b * h, lambda bi, hi: bi * h + hi)
    out = pl.pallas_call(
        _with_converts(_retention_kernel, 6, 1, len(jobs)),
        grid=(b, h),
        in_specs=[
            smem, smem,
            pl.BlockSpec((1, s, dk), lambda bi, hi: (bi, 0, hi)),
            pl.BlockSpec((1, s, dk), lambda bi, hi: (bi, 0, k_off + hi)),
            pl.BlockSpec((1, s, dv), lambda bi, hi: (bi, 0, v_off + hi)),
            pl.BlockSpec((1, s, dv), lambda bi, hi: (bi, 0, g_off + hi)),
        ] + job_in,
        out_specs=[pl.BlockSpec((1, s, dv), lambda bi, hi: (bi, 0, hi))] + job_out,
        out_shape=[jax.ShapeDtypeStruct((b, s, h * dv), _BF16)] + job_shapes,
        scratch_shapes=[
            pltpu.VMEM((s // RET_CHUNK, RET_CHUNK + 2 * dk, dv), _BF16),
            pltpu.VMEM((dk, dv), _F32),
            pltpu.VMEM((h, RET_CHUNK, RET_CHUNK), _F32),
            pltpu.VMEM((h, 4, RET_CHUNK, dk), _BF16),
        ],
        compiler_params=pltpu.CompilerParams(
            dimension_semantics=("arbitrary", "arbitrary"), vmem_limit_bytes=VMEM_LIMIT_BYTES),
        name="retention_core",
    )(log_decay_fwd, log_decay_bwd, proj, proj, proj, proj, *(w_f32 for w_f32, _ in jobs))
    return out[0], out[1:]


def _fill_alibi_bias(bias_ref):
    w, pair = WINDOW, ATT_PAIR
    second = lax.broadcasted_iota(jnp.int32, (1, pair * w), 1) >= w
    key = lax.broadcasted_iota(jnp.int32, (3 * w, pair * w), 0)
    qry = lax.broadcasted_iota(jnp.int32, (3 * w, pair * w), 1) & (w - 1)
    dist = jnp.abs(w + qry - key).astype(_F32)
    for i in range(ATT_Q_HEADS // pair):
        slope = [2.0 ** (-8.0 * (pair * i + j + 1) / ATT_Q_HEADS) for j in range(pair)]
        bias_ref[i] = (-jnp.where(second, slope[1], slope[0]) * dist) * LOG2E


def _attention_items(sink_ref, q_ref, k_ref, v_ref, bias_ref, first_block, write_block):
    w = WINDOW
    d = ATT_HEAD_DIM
    grp = ATT_GROUP
    pair = ATT_PAIR
    n_pairs = ATT_Q_HEADS // pair
    nb = k_ref.shape[1] // w

    lane_qry = lax.broadcasted_iota(jnp.int32, (1, pair * w), 1)
    second = lane_qry >= w

    def rows(i):
        return pl.ds(pl.multiple_of(i * w, w), w)

    key_minus_qry = lax.broadcasted_iota(jnp.int32, (w, pair * w), 0) - (lane_qry & (w - 1))
    never = 2 * w
    ones_pad = jnp.ones((w, ATT_ONES_PAD), _BF16)

    class Block:
        def __init__(self, j):
            n = first_block + j
            self.j = j
            self.q_rows = slice(j * w, (j + 1) * w)
            self.key_rows = [rows(jnp.maximum(n - 1, 0)), rows(n), rows(jnp.minimum(n + 1, nb - 1))]
            self.prev_bound = jnp.where(n > 0, 0, never)
            self.next_bound = jnp.where(n < nb - 1, 0, never)
            self.v_ones = [[jnp.concatenate([v_ref[0, r, h * d:(h + 1) * d], ones_pad], axis=1)
                            for r in self.key_rows] for h in range(ATT_KV_HEADS)]

    def scores(blk, i):
        h = i * pair // grp
        ks = slice(h * d, (h + 1) * d)
        k_band = jnp.concatenate([k_ref[0, r, ks] for r in blk.key_rows], axis=0)
        q_pair = jnp.concatenate([q_ref[0, blk.q_rows, hd * d:(hd + 1) * d]
                                  for hd in range(pair * i, pair * (i + 1))], axis=0)
        q_pair = (q_pair.astype(_F32) * (d ** -0.5 * LOG2E)).astype(_BF16)
        return lax.dot_general(k_band, q_pair, (((1,), (1,)), ((), ())), preferred_element_type=_F32)

    def attend(blk, i, s):
        h = i * pair // grp
        sink = jnp.where(second, sink_ref[pair * i + 1], sink_ref[pair * i]) * LOG2E
        maxes, accs = [], []
        for c in range(3):
            t = s[c * w:(c + 1) * w] + bias_ref[i, c * w:(c + 1) * w, :]
            if c == 0:
                t = jnp.where(key_minus_qry >= blk.prev_bound, t, NEG_INF * LOG2E)
            if c == 2:
                t = jnp.where(-key_minus_qry >= blk.next_bound, t, NEG_INF * LOG2E)
            m = jnp.maximum(jnp.max(t, axis=0, keepdims=True), sink)
            p = jnp.exp2(t - m)
            maxes.append(m)
            accs.append(lax.dot_general(blk.v_ones[h][c], p.astype(_BF16), (((0,), (0,)), ((), ())),
                                        preferred_element_type=_F32))
        m_all = jnp.maximum(jnp.maximum(maxes[0], maxes[1]), maxes[2])
        weights = [jnp.exp2(m - m_all) for m in maxes]
        total = accs[0] * weights[0] + accs[1] * weights[1] + accs[2] * weights[2]
        return total[0:d] / (total[d:d + 1] + jnp.exp2(sink - m_all))

    blocks = [Block(j) for j in range(ATT_BLOCKS_PER_STEP)]
    work = [(blk, i) for blk in blocks for i in range(n_pairs)]
    pending = [scores(*item) for item in work[:ATT_SCORES_AHEAD]]
    pieces = []

    def make_item(k):
        def run():
            blk, i = work[k]
            if k + ATT_SCORES_AHEAD < len(work):
                pending.append(scores(*work[k + ATT_SCORES_AHEAD]))
            out = attend(blk, i, pending.pop(0))
            pieces.extend(out[:, j * w:(j + 1) * w] for j in range(pair))
            if i == n_pairs - 1:
                write_block(blk.j, jnp.concatenate(pieces, axis=0).T)
                pieces.clear()
        return run

    return [make_item(k) for k in range(len(work))]


def _out_mlp_kernel(m_ref, x_ref, w_out_ref, g_post_mix_ref, g_pre_mlp_ref, w_up_ref, w_down_ref,
                    g_post_mlp_ref, g_next_ref, w_next_ref, o_ref, p_ref):
    n_sub = m_ref.shape[0] // MLP_SUBTILE
    rows = [slice(j * MLP_SUBTILE, (j + 1) * MLP_SUBTILE) for j in range(n_sub)]
    ys = [jnp.dot(m_ref[r, :], w_out_ref[...], preferred_element_type=_F32) for r in rows]
    xs, acts = [], []
    for r, y in zip(rows, ys):
        x = x_ref[r, :] + _rms(y, g_post_mix_ref[...])
        h = _rms(x, g_pre_mlp_ref[...]).astype(_BF16)
        xs.append(x)
        acts.append(jnp.dot(h, w_up_ref[...], preferred_element_type=_F32))
    zs = []
    for a in acts:
        a = jnp.maximum(a, 0.0)
        zs.append(jnp.dot((a * a).astype(_BF16), w_down_ref[...], preferred_element_type=_F32))
    outs = []
    for r, x, z in zip(rows, xs, zs):
        outs.append(x + _rms(z, g_post_mlp_ref[...]))
        o_ref[r, :] = outs[-1]
    for r, out in zip(rows, outs):
        h = _rms(out, g_next_ref[...]).astype(_BF16)
        p_ref[r, :] = jnp.dot(h, w_next_ref[...], preferred_element_type=_F32).astype(p_ref.dtype)


def _out_mlp(mixed, x2d, layer, w_out, g_post_mix, g_pre_mlp, w_up, w_down, g_post_mlp, g_pre_mix, w_next, jobs=()):
    t, d = x2d.shape
    dm = mixed.shape[1]
    n_next = w_next.shape[2]
    n_steps = t // MLP_TILE
    job_in, job_out, job_shapes = _convert_specs(jobs, n_steps, lambda i: i)
    out = pl.pallas_call(
        _with_converts(_out_mlp_kernel, 10, 2, len(jobs)),
        grid=(n_steps,),
        in_specs=[
            pl.BlockSpec((MLP_TILE, dm), lambda i: (i, 0)),
            pl.BlockSpec((MLP_TILE, d), lambda i: (i, 0)),
            _resident(w_out, 0),
            _resident(g_post_mix, layer),
            _resident(g_pre_mlp, layer),
            _resident(w_up, 0),
            _resident(w_down, 0),
            _resident(g_post_mlp, layer),
            _resident(g_pre_mix, layer + 1),
            _resident(w_next, 0),
        ] + job_in,
        out_specs=[pl.BlockSpec((MLP_TILE, d), lambda i: (i, 0)),
                   pl.BlockSpec((MLP_TILE, n_next), lambda i: (i, 0))] + job_out,
        out_shape=[jax.ShapeDtypeStruct((t, d), _F32), jax.ShapeDtypeStruct((t, n_next), _BF16)] + job_shapes,
        compiler_params=pltpu.CompilerParams(
            dimension_semantics=("arbitrary",), vmem_limit_bytes=VMEM_LIMIT_BYTES),
        name="out_mlp",
    )(mixed, x2d, w_out, g_post_mix, g_pre_mlp, w_up, w_down, g_post_mlp, g_pre_mix, w_next,
      *(w_f32 for w_f32, _ in jobs))
    return out[0], out[1], out[2:]


def _attn_mlp_kernel(sink_ref, q_ref, k_ref, v_ref, x_ref, w_out_ref, g_post_mix_ref, g_pre_mlp_ref,
                     w_up_ref, w_down_ref, g_post_mlp_ref, o_ref, bias_ref, mixed_ref):
    t = pl.program_id(0)
    tiles_per_batch = k_ref.shape[1] // ATT_TILE
    n_tiles = pl.num_programs(0) - 1
    tile = jnp.minimum(t, n_tiles - 1)
    slot = t % 2

    @pl.when(t == 0)
    def _init():
        _fill_alibi_bias(bias_ref)
        mixed_ref[1] = jnp.zeros_like(mixed_ref[1])

    def write_block(j, value):
        mixed_ref[slot, j * WINDOW:(j + 1) * WINDOW, :] = value.astype(mixed_ref.dtype)

    items = _attention_items(sink_ref, q_ref, k_ref, v_ref, bias_ref,
                             (tile % tiles_per_batch) * ATT_BLOCKS_PER_STEP, write_block)

    d_ff = w_up_ref.shape[1]
    d = w_down_ref.shape[1]
    n_sub = ATT_TILE // MLP_SUBTILE
    sub_rows = [slice(j * MLP_SUBTILE, (j + 1) * MLP_SUBTILE) for j in range(n_sub)]
    vals = [{} for _ in range(n_sub)]

    def out_proj(j):
        def run():
            vals[j]["y"] = jnp.dot(mixed_ref[1 - slot, sub_rows[j], :], w_out_ref[...], preferred_element_type=_F32)
        return run

    def up(j, c):
        def run():
            v = vals[j]
            if c == 0:
                v["x"] = x_ref[sub_rows[j], :] + _rms(v["y"], g_post_mix_ref[...])
                v["h"] = _rms(v["x"], g_pre_mlp_ref[...]).astype(_BF16)
                v["a"] = []
            cols = slice(c * MLP_UP_CHUNK, (c + 1) * MLP_UP_CHUNK)
            a = jnp.maximum(jnp.dot(v["h"], w_up_ref[:, cols], preferred_element_type=_F32), 0.0)
            v["a"].append((a * a).astype(_BF16))
        return run

    def down(j, c):
        def run():
            v = vals[j]
            if c == 0:
                v["act"] = jnp.concatenate(v["a"], axis=1)
                v["z"] = []
            cols = slice(c * MLP_DOWN_CHUNK, (c + 1) * MLP_DOWN_CHUNK)
            v["z"].append(jnp.dot(v["act"], w_down_ref[:, cols], preferred_element_type=_F32))
        return run

    def finish(j):
        def run():
            z = jnp.concatenate(vals[j]["z"], axis=1)
            o_ref[sub_rows[j], :] = vals[j]["x"] + _rms(z, g_post_mlp_ref[...])
        return run

    n_up, n_down = d_ff // MLP_UP_CHUNK, d // MLP_DOWN_CHUNK
    pieces = ([(out_proj(j), w_out_ref.shape[0] * d) for j in range(n_sub)]
              + [(up(j, c), d * MLP_UP_CHUNK) for j in range(n_sub) for c in range(n_up)])
    for j in range(n_sub):
        pieces += [(down(j, c), d_ff * MLP_DOWN_CHUNK) for c in range(n_down)]
        if j + 1 < n_sub:
            pieces.append((finish(j), 0))
    spread, tail = items[:len(items) - ATT_TAIL_ITEMS], items[len(items) - ATT_TAIL_ITEMS:]
    total, done, acc = sum(weight for _, weight in pieces), 0, 0
    for piece, weight in pieces:
        piece()
        acc += weight
        target = min(len(spread), -(-acc * len(spread) // total))
        for item in spread[done:target]:
            item()
        done = target
    for item in tail[:len(tail) // 2]:
        item()
    finish(n_sub - 1)()
    for item in tail[len(tail) // 2:]:
        item()


def _attn_mlp(proj, sink, x2d, layer, w_out, g_post_mix, g_pre_mlp, w_up, w_down, g_post_mlp, jobs=()):
    b, s, _ = proj.shape
    t, d = x2d.shape
    nq = ATT_Q_HEADS * ATT_HEAD_DIM
    nk = ATT_KV_HEADS * ATT_HEAD_DIM
    tiles_per_batch = s // ATT_TILE
    n_tiles = b * tiles_per_batch

    def att_tile(i):
        return jnp.minimum(i, n_tiles - 1)

    def mlp_tile(i):
        return jnp.maximum(i - 1, 0)

    job_in, job_out, job_shapes = _convert_specs(jobs, n_tiles, att_tile)
    out = pl.pallas_call(
        _with_converts(_attn_mlp_kernel, 11, 1, len(jobs)),
        grid=(n_tiles + 1,),
        in_specs=[
            pl.BlockSpec(memory_space=pltpu.SMEM),
            pl.BlockSpec((1, ATT_TILE, nq),
                         lambda i: (att_tile(i) // tiles_per_batch, att_tile(i) % tiles_per_batch, 0)),
            pl.BlockSpec((1, s, nk), lambda i: (att_tile(i) // tiles_per_batch, 0, nq // nk)),
            pl.BlockSpec((1, s, nk), lambda i: (att_tile(i) // tiles_per_batch, 0, nq // nk + 1)),
            pl.BlockSpec((ATT_TILE, d), lambda i: (mlp_tile(i), 0)),
            _resident(w_out, 0),
            _resident(g_post_mix, layer),
            _resident(g_pre_mlp, layer),
            _resident(w_up, 0),
            _resident(w_down, 0),
            _resident(g_post_mlp, layer),
        ] + job_in,
        out_specs=[pl.BlockSpec((ATT_TILE, d), lambda i: (mlp_tile(i), 0))] + job_out,
        out_shape=[jax.ShapeDtypeStruct((t, d), _F32)] + job_shapes,
        scratch_shapes=[
            pltpu.VMEM((ATT_Q_HEADS // ATT_PAIR, 3 * WINDOW, ATT_PAIR * WINDOW), _F32),
            pltpu.VMEM((2, ATT_TILE, nq), _BF16),
        ],
        compiler_params=pltpu.CompilerParams(
            dimension_semantics=("arbitrary",), vmem_limit_bytes=VMEM_LIMIT_BYTES),
        name="attn_mlp",
    )(sink, proj, proj, proj, x2d, w_out, g_post_mix, g_pre_mlp, w_up, w_down, g_post_mlp,
      *(w_f32 for w_f32, _ in jobs))
    return out[0], out[1:]


def kernel(x, w_in_ret, w_out_ret, log_decay_fwd, log_decay_bwd, w_in_attn, w_out_attn, sink_logits,
           g_pre_mix, g_post_mix, g_pre_mlp, g_post_mlp, w_up, w_down):
    b, s, d = x.shape
    x2d = x.reshape(b * s, d)
    g_pre_mix, g_post_mix, g_pre_mlp, g_post_mlp = (
        g.reshape(DEPTH, 1, d) for g in (g_pre_mix, g_post_mix, g_pre_mlp, g_post_mlp))
    w_in = w_in_ret[0:1].astype(_BF16)
    for layer in range(0, DEPTH, 2):
        i = layer // 2
        proj, (w_out, w_up_a, w_down_a) = _norm_proj(
            x2d, g_pre_mix, layer, w_in, 0, jobs=[(w_out_ret, i), (w_up, layer), (w_down, layer)])
        mixed, (w_in_b, w_out_b) = _retention_core(
            proj.reshape(b, s, -1), log_decay_fwd[i], log_decay_bwd[i], jobs=[(w_in_attn, i), (w_out_attn, i)])
        x2d, proj, (w_up_b, w_down_b) = _out_mlp(
            mixed.reshape(b * s, -1), x2d, layer, w_out, g_post_mix, g_pre_mlp, w_up_a, w_down_a, g_post_mlp,
            g_pre_mix, w_in_b, jobs=[(w_up, layer + 1), (w_down, layer + 1)])
        more = layer + 2 < DEPTH
        x2d, next_w_in = _attn_mlp(
            proj.reshape(b, s, -1), sink_logits[i], x2d, layer + 1, w_out_b, g_post_mix, g_pre_mlp, w_up_b,
            w_down_b, g_post_mlp, jobs=[(w_in_ret, i + 1)] if more else [])
        if more:
            (w_in,) = next_w_in
    return x2d.reshape(b, s, d)
```

```python
import jax
import jax.numpy as jnp
from jax import lax
from jax.experimental import pallas as pl
from jax.experimental.pallas import tpu as pltpu

D_MODEL = 1024
DEPTH = 4
RET_HEADS = 4
RET_QK_DIM = 256
RET_V_DIM = 512
ATT_Q_HEADS = 16
ATT_KV_HEADS = 4
ATT_GROUP = ATT_Q_HEADS // ATT_KV_HEADS
ATT_HEAD_DIM = 64
WINDOW = 128
D_FF = 4 * D_MODEL
EPS = 1e-6
NEG_INF = -1e30
LOG2E = 1.4426950408889634

TOKEN_TILE = 1024
PROJ_SUBTILE = 256
MLP_TILE = 512
MLP_SUBTILE = 256
RET_CHUNK = 256
RET_SCORES_AHEAD = 4
ATT_PAIR = 2
ATT_ONES_PAD = 16
ATT_BLOCKS_PER_STEP = 4
ATT_TILE = ATT_BLOCKS_PER_STEP * WINDOW
ATT_SCORES_AHEAD = 2
ATT_TAIL_ITEMS = 2
MLP_UP_CHUNK = 512
MLP_DOWN_CHUNK = 256
VMEM_LIMIT_BYTES = 56 * 1024 * 1024

_BF16 = jnp.bfloat16
_F32 = jnp.float32


def _rms(x, g):
    y = x * lax.rsqrt(jnp.mean(x * x, axis=-1, keepdims=True) + EPS)
    return y * g


def _resident(stacked, index):
    zeros = (0,) * (stacked.ndim - 1)
    return pl.BlockSpec((None,) + stacked.shape[1:], lambda *_: (index,) + zeros, pipeline_mode=pl.Buffered(1))


def _convert_specs(jobs, n_steps, step_of):
    in_specs, out_specs, out_shapes = [], [], []
    for w, index in jobs:
        _, k, n = w.shape
        rows = k // n_steps
        assert rows * n_steps == k and rows % 16 == 0, (w.shape, n_steps)
        in_specs.append(pl.BlockSpec((None, rows, n), lambda *g, index=index: (index, step_of(*g), 0)))
        out_specs.append(pl.BlockSpec((None, rows, n), lambda *g: (0, step_of(*g), 0)))
        out_shapes.append(jax.ShapeDtypeStruct((1, k, n), _BF16))
    return in_specs, out_specs, out_shapes


def _with_converts(body, n_in, n_out, n_jobs):
    def kernel(*refs):
        ins, rest = refs[:n_in], refs[n_in:]
        job_ins, rest = rest[:n_jobs], rest[n_jobs:]
        outs, rest = rest[:n_out], rest[n_out:]
        job_outs, scratch = rest[:n_jobs], rest[n_jobs:]
        body(*ins, *outs, *scratch)
        for src, dst in zip(job_ins, job_outs):
            dst[...] = src[...].astype(dst.dtype)
    return kernel


def _norm_proj_kernel(x_ref, g_ref, w_ref, o_ref):
    for j in range(x_ref.shape[0] // PROJ_SUBTILE):
        r = slice(j * PROJ_SUBTILE, (j + 1) * PROJ_SUBTILE)
        h = _rms(x_ref[r, :], g_ref[...]).astype(_BF16)
        o_ref[r, :] = jnp.dot(h, w_ref[...], preferred_element_type=_F32).astype(o_ref.dtype)


def _norm_proj(x2d, g, layer, w, w_index, jobs=()):
    t, d = x2d.shape
    n = w.shape[2]
    n_steps = t // TOKEN_TILE
    job_in, job_out, job_shapes = _convert_specs(jobs, n_steps, lambda i: i)
    out = pl.pallas_call(
        _with_converts(_norm_proj_kernel, 3, 1, len(jobs)),
        grid=(n_steps,),
        in_specs=[
            pl.BlockSpec((TOKEN_TILE, d), lambda i: (i, 0)),
            _resident(g, layer),
            _resident(w, w_index),
        ] + job_in,
        out_specs=[pl.BlockSpec((TOKEN_TILE, n), lambda i: (i, 0))] + job_out,
        out_shape=[jax.ShapeDtypeStruct((t, n), _BF16)] + job_shapes,
        compiler_params=pltpu.CompilerParams(
            dimension_semantics=("arbitrary",), vmem_limit_bytes=VMEM_LIMIT_BYTES),
        name="norm_proj",
    )(x2d, g, w, *(w_f32 for w_f32, _ in jobs))
    return out[0], out[1:]


def _retention_kernel(lgf_ref, lgb_ref, q_ref, k_ref, v_ref, g_ref, o_ref, rhs_ref, state_ref, inner_ref, decay_ref):
    c = RET_CHUNK
    dk = RET_QK_DIM
    seq = q_ref.shape[1]
    n_chunks = seq // c
    head = pl.program_id(1)
    lf = lgf_ref[head]
    lb = lgb_ref[head]
    k_scale = dk ** -0.5

    @pl.when(pl.program_id(0) == 0)
    def _build_tables():
        row = lax.broadcasted_iota(jnp.int32, (c, c), 0)
        col = lax.broadcasted_iota(jnp.int32, (c, c), 1)
        diff = (row - col).astype(_F32)
        inner_ref[head] = jnp.where(
            diff >= 0.0,
            jnp.exp(lf * jnp.maximum(diff, 0.0)),
            jnp.exp(lb * jnp.maximum(-diff, 0.0))) * k_scale
        idx = lax.broadcasted_iota(jnp.int32, (c, dk), 0).astype(_F32)
        decay_ref[head, 0] = jnp.exp(lf * (idx + 1.0)).astype(_BF16)
        decay_ref[head, 1] = (jnp.exp(lf * (c - 1.0 - idx)) * k_scale).astype(_BF16)
        decay_ref[head, 2] = jnp.exp(lb * (c - idx)).astype(_BF16)
        decay_ref[head, 3] = (jnp.exp(lb * idx) * k_scale).astype(_BF16)

    q_decay_f, k_decay_f, q_decay_b, k_decay_b = (decay_ref.at[head, j] for j in range(4))
    chunk_decay_f = jnp.exp(lf * c)
    chunk_decay_b = jnp.exp(lb * c)

    def rows(i):
        return slice(i * c, (i + 1) * c)

    def kv_product(i, k_decay):
        return lax.dot_general(k_ref[0, rows(i), :] * k_decay[...], v_ref[0, rows(i), :], (((0,), (0,)), ((), ())),
                               preferred_element_type=_F32)

    state_ref[...] = jnp.zeros_like(state_ref)
    for i in range(n_chunks):
        rhs_ref[i, 0:c, :] = v_ref[0, rows(i), :]
        rhs_ref[i, c:c + dk, :] = state_ref[...].astype(_BF16)
        if i + 1 < n_chunks:
            state_ref[...] = state_ref[...] * chunk_decay_f + kv_product(i, k_decay_f)
    state_ref[...] = jnp.zeros_like(state_ref)
    for i in reversed(range(n_chunks)):
        rhs_ref[i, c + dk:c + 2 * dk, :] = state_ref[...].astype(_BF16)
        if i > 0:
            state_ref[...] = state_ref[...] * chunk_decay_b + kv_product(i, k_decay_b)

    def chunk_scores(i):
        return lax.dot_general(q_ref[0, rows(i), :], k_ref[0, rows(i), :], (((1,), (1,)), ((), ())),
                               preferred_element_type=_F32)

    pending = [chunk_scores(i) for i in range(min(RET_SCORES_AHEAD, n_chunks))]
    for i in range(n_chunks):
        if i + RET_SCORES_AHEAD < n_chunks:
            pending.append(chunk_scores(i + RET_SCORES_AHEAD))
        s = pending.pop(0)
        q = q_ref[0, rows(i), :]
        lhs = jnp.concatenate([(s * inner_ref[head]).astype(_BF16), q * q_decay_f[...], q * q_decay_b[...]], axis=1)
        o = jnp.dot(lhs, rhs_ref[i], preferred_element_type=_F32)
        o = o * lax.rsqrt(jnp.mean(o * o, axis=-1, keepdims=True) + EPS)
        half = (g_ref[0, rows(i), :] * jnp.asarray(0.5, _BF16)).astype(_F32)
        silu = half + half * jnp.tanh(half)
        o_ref[0, rows(i), :] = (silu * o).astype(o_ref.dtype)


def _retention_core(proj, log_decay_fwd, log_decay_bwd, jobs=()):
    b, s, _ = proj.shape
    dk, dv, h = RET_QK_DIM, RET_V_DIM, RET_HEADS
    k_off = h * dk // dk
    v_off = 2 * h * dk // dv
    g_off = (2 * h * dk + h * dv) // dv
    smem = pl.BlockSpec(memory_space=pltpu.SMEM)
    job_in, job_out, job_shapes = _convert_specs(jobs, b * h, lambda bi, hi: bi * h + hi)
    out = pl.pallas_call(
        _with_converts(_retention_kernel, 6, 1, len(jobs)),
        grid=(b, h),
        in_specs=[
            smem, smem,
            pl.BlockSpec((1, s, dk), lambda bi, hi: (bi, 0, hi)),
            pl.BlockSpec((1, s, dk), lambda bi, hi: (bi, 0, k_off + hi)),
            pl.BlockSpec((1, s, dv), lambda bi, hi: (bi, 0, v_off + hi)),
            pl.BlockSpec((1, s, dv), lambda bi, hi: (bi, 0, g_off + hi)),
        ] + job_in,
        out_specs=[pl.BlockSpec((1, s, dv), lambda bi, hi: (bi, 0, hi))] + job_out,
        out_shape=[jax.ShapeDtypeStruct((b, s, h * dv), _BF16)] + job_shapes,
        scratch_shapes=[
            pltpu.VMEM((s // RET_CHUNK, RET_CHUNK + 2 * dk, dv), _BF16),
            pltpu.VMEM((dk, dv), _F32),
            pltpu.VMEM((h, RET_CHUNK, RET_CHUNK), _F32),
            pltpu.VMEM((h, 4, RET_CHUNK, dk), _BF16),
        ],
        compiler_params=pltpu.CompilerParams(
            dimension_semantics=("arbitrary", "arbitrary"), vmem_limit_bytes=VMEM_LIMIT_BYTES),
        name="retention_core",
    )(log_decay_fwd, log_decay_bwd, proj, proj, proj, proj, *(w_f32 for w_f32, _ in jobs))
    return out[0], out[1:]


def _fill_alibi_bias(bias_ref):
    w, pair = WINDOW, ATT_PAIR
    second = lax.broadcasted_iota(jnp.int32, (1, pair * w), 1) >= w
    key = lax.broadcasted_iota(jnp.int32, (3 * w, pair * w), 0)
    qry = lax.broadcasted_iota(jnp.int32, (3 * w, pair * w), 1) & (w - 1)
    dist = jnp.abs(w + qry - key).astype(_F32)
    for i in range(ATT_Q_HEADS // pair):
        slope = [2.0 ** (-8.0 * (pair * i + j + 1) / ATT_Q_HEADS) for j in range(pair)]
        bias_ref[i] = (-jnp.where(second, slope[1], slope[0]) * dist) * LOG2E


def _attention_items(sink_ref, q_ref, k_ref, v_ref, bias_ref, first_block, write_block):
    w = WINDOW
    d = ATT_HEAD_DIM
    grp = ATT_GROUP
    pair = ATT_PAIR
    n_pairs = ATT_Q_HEADS // pair
    nb = k_ref.shape[1] // w

    lane_qry = lax.broadcasted_iota(jnp.int32, (1, pair * w), 1)
    second = lane_qry >= w

    def rows(i):
        return pl.ds(pl.multiple_of(i * w, w), w)

    key_minus_qry = lax.broadcasted_iota(jnp.int32, (w, pair * w), 0) - (lane_qry & (w - 1))
    never = 2 * w
    ones_pad = jnp.ones((w, ATT_ONES_PAD), _BF16)

    class Block:
        def __init__(self, j):
            n = first_block + j
            self.j = j
            self.q_rows = slice(j * w, (j + 1) * w)
            self.key_rows = [rows(jnp.maximum(n - 1, 0)), rows(n), rows(jnp.minimum(n + 1, nb - 1))]
            self.prev_bound = jnp.where(n > 0, 0, never)
            self.next_bound = jnp.where(n < nb - 1, 0, never)
            self.v_ones = [[jnp.concatenate([v_ref[0, r, h * d:(h + 1) * d], ones_pad], axis=1)
                            for r in self.key_rows] for h in range(ATT_KV_HEADS)]

    def scores(blk, i):
        h = i * pair // grp
        ks = slice(h * d, (h + 1) * d)
        k_band = jnp.concatenate([k_ref[0, r, ks] for r in blk.key_rows], axis=0)
        q_pair = jnp.concatenate([q_ref[0, blk.q_rows, hd * d:(hd + 1) * d]
                                  for hd in range(pair * i, pair * (i + 1))], axis=0)
        q_pair = (q_pair.astype(_F32) * (d ** -0.5 * LOG2E)).astype(_BF16)
        return lax.dot_general(k_band, q_pair, (((1,), (1,)), ((), ())), preferred_element_type=_F32)

    def attend(blk, i, s):
        h = i * pair // grp
        sink = jnp.where(second, sink_ref[pair * i + 1], sink_ref[pair * i]) * LOG2E
        maxes, accs = [], []
        for c in range(3):
            t = s[c * w:(c + 1) * w] + bias_ref[i, c * w:(c + 1) * w, :]
            if c == 0:
                t = jnp.where(key_minus_qry >= blk.prev_bound, t, NEG_INF * LOG2E)
            if c == 2:
                t = jnp.where(-key_minus_qry >= blk.next_bound, t, NEG_INF * LOG2E)
            m = jnp.maximum(jnp.max(t, axis=0, keepdims=True), sink)
            p = jnp.exp2(t - m)
            maxes.append(m)
            accs.append(lax.dot_general(blk.v_ones[h][c], p.astype(_BF16), (((0,), (0,)), ((), ())),
                                        preferred_element_type=_F32))
        m_all = jnp.maximum(jnp.maximum(maxes[0], maxes[1]), maxes[2])
        weights = [jnp.exp2(m - m_all) for m in maxes]
        total = accs[0] * weights[0] + accs[1] * weights[1] + accs[2] * weights[2]
        return total[0:d] / (total[d:d + 1] + jnp.exp2(sink - m_all))

    blocks = [Block(j) for j in range(ATT_BLOCKS_PER_STEP)]
    work = [(blk, i) for blk in blocks for i in range(n_pairs)]
    pending = [scores(*item) for item in work[:ATT_SCORES_AHEAD]]
    pieces = []

    def make_item(k):
        def run():
            blk, i = work[k]
            if k + ATT_SCORES_AHEAD < len(work):
                pending.append(scores(*work[k + ATT_SCORES_AHEAD]))
            out = attend(blk, i, pending.pop(0))
            pieces.extend(out[:, j * w:(j + 1) * w] for j in range(pair))
            if i == n_pairs - 1:
                write_block(blk.j, jnp.concatenate(pieces, axis=0).T)
                pieces.clear()
        return run

    return [make_item(k) for k in range(len(work))]


def _out_mlp_kernel(m_ref, x_ref, w_out_ref, g_post_mix_ref, g_pre_mlp_ref, w_up_ref, w_down_ref,
                    g_post_mlp_ref, g_next_ref, w_next_ref, o_ref, p_ref):
    n_sub = m_ref.shape[0] // MLP_SUBTILE
    rows = [slice(j * MLP_SUBTILE, (j + 1) * MLP_SUBTILE) for j in range(n_sub)]
    ys = [jnp.dot(m_ref[r, :], w_out_ref[...], preferred_element_type=_F32) for r in rows]
    xs, acts = [], []
    for r, y in zip(rows, ys):
        x = x_ref[r, :] + _rms(y, g_post_mix_ref[...])
        h = _rms(x, g_pre_mlp_ref[...]).astype(_BF16)
        xs.append(x)
        acts.append(jnp.dot(h, w_up_ref[...], preferred_element_type=_F32))
    zs = []
    for a in acts:
        a = jnp.maximum(a, 0.0)
        zs.append(jnp.dot((a * a).astype(_BF16), w_down_ref[...], preferred_element_type=_F32))
    outs = []
    for r, x, z in zip(rows, xs, zs):
        outs.append(x + _rms(z, g_post_mlp_ref[...]))
        o_ref[r, :] = outs[-1]
    for r, out in zip(rows, outs):
        h = _rms(out, g_next_ref[...]).astype(_BF16)
        p_ref[r, :] = jnp.dot(h, w_next_ref[...], preferred_element_type=_F32).astype(p_ref.dtype)


def _out_mlp(mixed, x2d, layer, w_out, g_post_mix, g_pre_mlp, w_up, w_down, g_post_mlp, g_pre_mix, w_next, jobs=()):
    t, d = x2d.shape
    dm = mixed.shape[1]
    n_next = w_next.shape[2]
    n_steps = t // MLP_TILE
    job_in, job_out, job_shapes = _convert_specs(jobs, n_steps, lambda i: i)
    out = pl.pallas_call(
        _with_converts(_out_mlp_kernel, 10, 2, len(jobs)),
        grid=(n_steps,),
        in_specs=[
            pl.BlockSpec((MLP_TILE, dm), lambda i: (i, 0)),
            pl.BlockSpec((MLP_TILE, d), lambda i: (i, 0)),
            _resident(w_out, 0),
            _resident(g_post_mix, layer),
            _resident(g_pre_mlp, layer),
            _resident(w_up, 0),
            _resident(w_down, 0),
            _resident(g_post_mlp, layer),
            _resident(g_pre_mix, layer + 1),
            _resident(w_next, 0),
        ] + job_in,
        out_specs=[pl.BlockSpec((MLP_TILE, d), lambda i: (i, 0)),
                   pl.BlockSpec((MLP_TILE, n_next), lambda i: (i, 0))] + job_out,
        out_shape=[jax.ShapeDtypeStruct((t, d), _F32), jax.ShapeDtypeStruct((t, n_next), _BF16)] + job_shapes,
        compiler_params=pltpu.CompilerParams(
            dimension_semantics=("arbitrary",), vmem_limit_bytes=VMEM_LIMIT_BYTES),
        name="out_mlp",
    )(mixed, x2d, w_out, g_post_mix, g_pre_mlp, w_up, w_down, g_post_mlp, g_pre_mix, w_next,
      *(w_f32 for w_f32, _ in jobs))
    return out[0], out[1], out[2:]


def _attn_mlp_kernel(sink_ref, q_ref, k_ref, v_ref, x_ref, w_out_ref, g_post_mix_ref, g_pre_mlp_ref,
                     w_up_ref, w_down_ref, g_post_mlp_ref, o_ref, bias_ref, mixed_ref):
    t = pl.program_id(0)
    tiles_per_batch = k_ref.shape[1] // ATT_TILE
    n_tiles = pl.num_programs(0) - 1
    tile = jnp.minimum(t, n_tiles - 1)
    slot = t % 2

    @pl.when(t == 0)
    def _init():
        _fill_alibi_bias(bias_ref)
        mixed_ref[1] = jnp.zeros_like(mixed_ref[1])

    def write_block(j, value):
        mixed_ref[slot, j * WINDOW:(j + 1) * WINDOW, :] = value.astype(mixed_ref.dtype)

    items = _attention_items(sink_ref, q_ref, k_ref, v_ref, bias_ref,
                             (tile % tiles_per_batch) * ATT_BLOCKS_PER_STEP, write_block)

    d_ff = w_up_ref.shape[1]
    d = w_down_ref.shape[1]
    n_sub = ATT_TILE // MLP_SUBTILE
    sub_rows = [slice(j * MLP_SUBTILE, (j + 1) * MLP_SUBTILE) for j in range(n_sub)]
    vals = [{} for _ in range(n_sub)]

    def out_proj(j):
        def run():
            vals[j]["y"] = jnp.dot(mixed_ref[1 - slot, sub_rows[j], :], w_out_ref[...], preferred_element_type=_F32)
        return run

    def up(j, c):
        def run():
            v = vals[j]
            if c == 0:
                v["x"] = x_ref[sub_rows[j], :] + _rms(v["y"], g_post_mix_ref[...])
                v["h"] = _rms(v["x"], g_pre_mlp_ref[...]).astype(_BF16)
                v["a"] = []
            cols = slice(c * MLP_UP_CHUNK, (c + 1) * MLP_UP_CHUNK)
            a = jnp.maximum(jnp.dot(v["h"], w_up_ref[:, cols], preferred_element_type=_F32), 0.0)
            v["a"].append((a * a).astype(_BF16))
        return run

    def down(j, c):
        def run():
            v = vals[j]
            if c == 0:
                v["act"] = jnp.concatenate(v["a"], axis=1)
                v["z"] = []
            cols = slice(c * MLP_DOWN_CHUNK, (c + 1) * MLP_DOWN_CHUNK)
            v["z"].append(jnp.dot(v["act"], w_down_ref[:, cols], preferred_element_type=_F32))
        return run

    def finish(j):
        def run():
            z = jnp.concatenate(vals[j]["z"], axis=1)
            o_ref[sub_rows[j], :] = vals[j]["x"] + _rms(z, g_post_mlp_ref[...])
        return run

    n_up, n_down = d_ff // MLP_UP_CHUNK, d // MLP_DOWN_CHUNK
    pieces = ([(out_proj(j), w_out_ref.shape[0] * d) for j in range(n_sub)]
              + [(up(j, c), d * MLP_UP_CHUNK) for j in range(n_sub) for c in range(n_up)])
    for j in range(n_sub):
        pieces += [(down(j, c), d_ff * MLP_DOWN_CHUNK) for c in range(n_down)]
        if j + 1 < n_sub:
            pieces.append((finish(j), 0))
    spread, tail = items[:len(items) - ATT_TAIL_ITEMS], items[len(items) - ATT_TAIL_ITEMS:]
    total, done, acc = sum(weight for _, weight in pieces), 0, 0
    for piece, weight in pieces:
        piece()
        acc += weight
        target = min(len(spread), -(-acc * len(spread) // total))
        for item in spread[done:target]:
            item()
        done = target
    for item in tail[:len(tail) // 2]:
        item()
    finish(n_sub - 1)()
    for item in tail[len(tail) // 2:]:
        item()


def _attn_mlp(proj, sink, x2d, layer, w_out, g_post_mix, g_pre_mlp, w_up, w_down, g_post_mlp, jobs=()):
    b, s, _ = proj.shape
    t, d = x2d.shape
    nq = ATT_Q_HEADS * ATT_HEAD_DIM
    nk = ATT_KV_HEADS * ATT_HEAD_DIM
    tiles_per_batch = s // ATT_TILE
    n_tiles = b * tiles_per_batch

    def att_tile(i):
        return jnp.minimum(i, n_tiles - 1)

    def mlp_tile(i):
        return jnp.maximum(i - 1, 0)

    job_in, job_out, job_shapes = _convert_specs(jobs, n_tiles, att_tile)
    out = pl.pallas_call(
        _with_converts(_attn_mlp_kernel, 11, 1, len(jobs)),
        grid=(n_tiles + 1,),
        in_specs=[
            pl.BlockSpec(memory_space=pltpu.SMEM),
            pl.BlockSpec((1, ATT_TILE, nq),
                         lambda i: (att_tile(i) // tiles_per_batch, att_tile(i) % tiles_per_batch, 0)),
            pl.BlockSpec((1, s, nk), lambda i: (att_tile(i) // tiles_per_batch, 0, nq // nk)),
            pl.BlockSpec((1, s, nk), lambda i: (att_tile(i) // tiles_per_batch, 0, nq // nk + 1)),
            pl.BlockSpec((ATT_TILE, d), lambda i: (mlp_tile(i), 0)),
            _resident(w_out, 0),
            _resident(g_post_mix, layer),
            _resident(g_pre_mlp, layer),
            _resident(w_up, 0),
            _resident(w_down, 0),
            _resident(g_post_mlp, layer),
        ] + job_in,
        out_specs=[pl.BlockSpec((ATT_TILE, d), lambda i: (mlp_tile(i), 0))] + job_out,
        out_shape=[jax.ShapeDtypeStruct((t, d), _F32)] + job_shapes,
        scratch_shapes=[
            pltpu.VMEM((ATT_Q_HEADS // ATT_PAIR, 3 * WINDOW, ATT_PAIR * WINDOW), _F32),
            pltpu.VMEM((2, ATT_TILE, nq), _BF16),
        ],
        compiler_params=pltpu.CompilerParams(
            dimension_semantics=("arbitrary",), vmem_limit_bytes=VMEM_LIMIT_BYTES),
        name="attn_mlp",
    )(sink, proj, proj, proj, x2d, w_out, g_post_mix, g_pre_mlp, w_up, w_down, g_post_mlp,
      *(w_f32 for w_f32, _ in jobs))
    return out[0], out[1:]


def kernel(x, w_in_ret, w_out_ret, log_decay_fwd, log_decay_bwd, w_in_attn, w_out_attn, sink_logits,
           g_pre_mix, g_post_mix, g_pre_mlp, g_post_mlp, w_up, w_down):
    b, s, d = x.shape
    x2d = x.reshape(b * s, d)
    g_pre_mix, g_post_mix, g_pre_mlp, g_post_mlp = (
        g.reshape(DEPTH, 1, d) for g in (g_pre_mix, g_post_mix, g_pre_mlp, g_post_mlp))
    w_in = w_in_ret[0:1].astype(_BF16)
    for layer in range(0, DEPTH, 2):
        i = layer // 2
        proj, (w_out, w_up_a, w_down_a) = _norm_proj(
            x2d, g_pre_mix, layer, w_in, 0, jobs=[(w_out_ret, i), (w_up, layer), (w_down, layer)])
        mixed, (w_in_b, w_out_b) = _retention_core(
            proj.reshape(b, s, -1), log_decay_fwd[i], log_decay_bwd[i], jobs=[(w_in_attn, i), (w_out_attn, i)])
        x2d, proj, (w_up_b, w_down_b) = _out_mlp(
            mixed.reshape(b * s, -1), x2d, layer, w_out, g_post_mix, g_pre_mlp, w_up_a, w_down_a, g_post_mlp,
            g_pre_mix, w_in_b, jobs=[(w_up, layer + 1), (w_down, layer + 1)])
        more = layer + 2 < DEPTH
        x2d, next_w_in = _attn_mlp(
            proj.reshape(b, s, -1), sink_logits[i], x2d, layer + 1, w_out_b, g_post_mix, g_pre_mlp, w_up_b,
            w_down_b, g_post_mlp, jobs=[(w_in_ret, i + 1)] if more else [])
        if more:
            (w_in,) = next_w_in
    return x2d.reshape(b, s, d)
```
